```python
import math
import jax, jax.numpy as jnp
from jax import lax
import numpy as np

D_MODEL = 2048
BATCH = 4
SEQ = 2048
DEPTH = 2
DEC_BATCH = 128
DEC_SEQ = 4
PAST_LEN = 16384
PAGE_SIZE = 128

D_MIX = D_MODEL
ML_HEADS = 4
ML_DIM = D_MIX // 16
ML_WIDTH = ML_HEADS * ML_DIM
ML_CHUNK = 64
RW_HEAD = 64
RW_WIDTH = D_MIX // 2
RW_HEADS = RW_WIDTH // RW_HEAD
RW_LORA_W = 64
RW_LORA_A = 64
RW_LORA_G = 128
SG_WIDTH = D_MIX - ML_WIDTH - RW_WIDTH
SG_CHUNK = 128
SG_GROUP = 128
SG_GROUPS = SG_WIDTH // SG_GROUP
ML_COLS = 4 * ML_WIDTH + 2 * ML_HEADS
RW_COLS = 3 * RW_WIDTH + RW_LORA_W + RW_LORA_A + RW_LORA_G
SG_COLS = 2 * SG_WIDTH
N_IN = ML_COLS + RW_COLS + SG_COLS
D_FF = 5632
CONV_W = 3
EPS = 1e-6
RW_GN_EPS = RW_HEAD * 1e-5

kernel_name = 'hybrid_mlstm_rwkv7_sgu_convffn_step'


def rmsnorm(x, g):
    xf = x.astype(jnp.float32)
    y = xf * lax.rsqrt(jnp.mean(xf * xf, axis=-1, keepdims=True) + EPS)
    return (y * g.astype(jnp.float32)).astype(x.dtype)


def _heads(z, n_heads):
    return z.reshape(z.shape[:-1] + (n_heads, z.shape[-1] // n_heads))


def mlstm_mixer(p, b_i, b_f, norm_g, C0, n0, m0):
    bsz, T, _ = p.shape
    pf = p.astype(jnp.float32)
    q, k, v, o = [_heads(pf[..., j * ML_WIDTH:(j + 1) * ML_WIDTH], ML_HEADS) for j in range(4)]
    k = k * ML_DIM ** -0.5
    ig = pf[..., 4 * ML_WIDTH:4 * ML_WIDTH + ML_HEADS] + b_i
    lf = jax.nn.log_sigmoid(pf[..., 4 * ML_WIDTH + ML_HEADS:ML_COLS] + b_f)
    L = math.gcd(T, ML_CHUNK)
    nc = T // L

    def to_chunks(a):
        a = a.reshape((bsz, nc, L) + a.shape[2:])
        return jnp.moveaxis(a, (1, 3), (0, 2))

    causal = jnp.tril(jnp.ones((L, L), dtype=bool))

    def step(carry, xs):
        C, n, m = carry
        qc, kc, vc, igc, lfc = xs
        b = jnp.cumsum(lfc, axis=-1)
        inter = b + m[..., None]
        dmat = jnp.where(causal, b[..., :, None] - b[..., None, :] + igc[..., None, :], -jnp.inf)
        m_row = jnp.maximum(inter, jnp.max(dmat, axis=-1))
        s_inter = jnp.exp(inter - m_row)
        att = jnp.einsum('bhld,bhsd->bhls', qc, kc) * jnp.exp(dmat - m_row[..., None])
        num = s_inter[..., None] * jnp.einsum('bhld,bhde->bhle', qc, C) + jnp.einsum('bhls,bhse->bhle', att, vc)
        den = s_inter * jnp.einsum('bhld,bhd->bhl', qc, n) + jnp.sum(att, axis=-1)
        h = num / jnp.maximum(jnp.abs(den), jnp.exp(-m_row))[..., None]
        b_last = b[..., -1]
        g = b_last[..., None] - b + igc
        m_new = jnp.maximum(b_last + m, jnp.max(g, axis=-1))
        wk = jnp.exp(g - m_new[..., None])
        dec = jnp.exp(b_last + m - m_new)
        C = dec[..., None, None] * C + jnp.einsum('bhl,bhld,bhle->bhde', wk, kc, vc)
        n = dec[..., None] * n + jnp.einsum('bhl,bhld->bhd', wk, kc)
        return (C, n, m_new), h

    carry0 = (C0.astype(jnp.float32), n0.astype(jnp.float32), m0.astype(jnp.float32))
    xs = tuple(to_chunks(a) for a in (q, k, v, ig, lf))
    (C, n, m), hs = lax.scan(step, carry0, xs)
    h = jnp.moveaxis(hs, (0, 2), (1, 3)).reshape(bsz, T, ML_HEADS, ML_DIM)
    h = h * lax.rsqrt(jnp.mean(h * h, axis=-1, keepdims=True) + EPS)
    out = h.reshape(bsz, T, ML_WIDTH) * norm_g * jax.nn.sigmoid(o.reshape(bsz, T, ML_WIDTH))
    return out, C, n, m


def rwkv7_mixer(p, shift0, S0, mu, w0, w_up, a0, a_up, g_up, k_k, k_a, r_k, ln_g, ln_b):
    bsz, T, _ = p.shape
    pf = p.astype(jnp.float32)
    prev = jnp.concatenate([shift0[:, None, :].astype(jnp.float32), pf[:, :-1]], axis=1)
    px = pf + (prev - pf) * mu
    o1, o2, o3 = RW_WIDTH, 2 * RW_WIDTH, 3 * RW_WIDTH
    o4 = o3 + RW_LORA_W
    o5 = o4 + RW_LORA_A
    r, k, v = px[..., :o1], px[..., o1:o2], px[..., o2:o3]
    w_log = -jax.nn.softplus(-(w0 + jnp.tanh(px[..., o3:o4]) @ w_up)) - 0.5
    decay = jnp.exp(-jnp.exp(w_log))
    a = jax.nn.sigmoid(a0 + px[..., o4:o5] @ a_up)
    g = jax.nn.sigmoid(px[..., o5:]) @ g_up
    kk = _heads(k * k_k, RW_HEADS)
    kk = kk * lax.rsqrt(jnp.maximum(jnp.sum(kk * kk, axis=-1, keepdims=True), 1e-24))
    k = k * (1.0 + (a - 1.0) * k_a)
    rh, wh, kh, vh, ah = (_heads(z, RW_HEADS) for z in (r, decay, k, v, a))
    a_vec = -kk
    b_vec = kk * ah

    def step(S, xs):
        rt, wt, kt, vt, at, bt = xs
        Sa = jnp.einsum('bhvk,bhk->bhv', S, at)
        S = S * wt[:, :, None, :] + Sa[..., None] * bt[:, :, None, :] + vt[..., None] * kt[:, :, None, :]
        return S, jnp.einsum('bhvk,bhk->bhv', S, rt)

    xs = tuple(jnp.moveaxis(z, 1, 0) for z in (rh, wh, kh, vh, a_vec, b_vec))
    S, ys = lax.scan(step, S0.astype(jnp.float32), xs)
    y = jnp.moveaxis(ys, 0, 1)
    mean = jnp.mean(y, axis=-1, keepdims=True)
    var = jnp.mean(jnp.square(y - mean), axis=-1, keepdims=True)
    y = ((y - mean) * lax.rsqrt(var + RW_GN_EPS)).reshape(bsz, T, RW_WIDTH) * ln_g + ln_b
    bonus = jnp.sum(rh * kh * r_k, axis=-1, keepdims=True) * vh
    out = (y + bonus.reshape(bsz, T, RW_WIDTH)) * g
    return out, S, pf[:, -1]


def sgu_mixer(p, ln_g, ln_b, w_s, b_s):
    bsz, T, _ = p.shape
    z = jax.nn.gelu(p.astype(jnp.float32), approximate=False)
    u, v = z[..., :SG_WIDTH], z[..., SG_WIDTH:]
    v = v.reshape(bsz, T, SG_GROUPS, SG_GROUP)
    mean = jnp.mean(v, axis=-1, keepdims=True)
    var = jnp.mean(jnp.square(v - mean), axis=-1, keepdims=True)
    v = (v - mean) * lax.rsqrt(var + EPS) * ln_g.reshape(SG_GROUPS, SG_GROUP) + ln_b.reshape(SG_GROUPS, SG_GROUP)
    L = min(T, SG_CHUNK)
    Tp = -(-T // L) * L
    vp = jnp.pad(v, ((0, 0), (0, Tp - T), (0, 0), (0, 0))).reshape(bsz, Tp // L, L, SG_GROUPS, SG_GROUP)
    W = jnp.tril(w_s[:, :L, :L])
    mixed = jnp.einsum('gts,bcsgd->bctgd', W, vp) + b_s[:, :L].T[None, None, :, :, None]
    mixed = mixed.reshape(bsz, Tp, SG_GROUPS, SG_GROUP)[:, :T].reshape(bsz, T, SG_WIDTH)
    return u * mixed, v.reshape(bsz, T, SG_WIDTH)


def conv_ffn(x, buf, w_up, conv_w, conv_b, w_down):
    T = x.shape[1]
    h = x @ w_up
    hp = jnp.concatenate([buf.astype(h.dtype), h], axis=1)
    hc = conv_b
    for j in range(CONV_W):
        hc = hc + conv_w[j] * hp[:, j:j + T]
    gate, val = hc[..., :D_FF], hc[..., D_FF:]
    y = (jax.nn.silu(gate) * val) @ w_down
    return y, hp[:, -(CONV_W - 1):]


def trunk(x, C0, n0, m0, S0, sh0, cb0, prm, return_v):
    Cs, ns, ms, Ss, shs, cbs, vs = [], [], [], [], [], [], []
    for l in range(DEPTH):
        h = rmsnorm(x, prm['mix_norm_g'][l])
        p = h @ prm['w_in'][l]
        ym, C, n, m = mlstm_mixer(p[..., :ML_COLS], prm['mlstm_b_i'][l], prm['mlstm_b_f'][l],
                                  prm['mlstm_norm_g'][l], C0[l], n0[l], m0[l])
        yr, S, sh = rwkv7_mixer(p[..., ML_COLS:ML_COLS + RW_COLS], sh0[l], S0[l],
                                prm['rwkv_mu'][l], prm['rwkv_w0'][l], prm['rwkv_w_up'][l],
                                prm['rwkv_a0'][l], prm['rwkv_a_up'][l], prm['rwkv_g_up'][l],
                                prm['rwkv_k_k'][l], prm['rwkv_k_a'][l], prm['rwkv_r_k'][l],
                                prm['rwkv_ln_g'][l], prm['rwkv_ln_b'][l])
        ys, vrows = sgu_mixer(p[..., ML_COLS + RW_COLS:], prm['sgu_ln_g'][l], prm['sgu_ln_b'][l],
                              prm['sgu_w'][l], prm['sgu_b'][l])
        mix = jnp.concatenate([ym, yr, ys], axis=-1).astype(x.dtype)
        x = x + mix @ prm['w_out'][l]
        f, cb = conv_ffn(rmsnorm(x, prm['ffn_norm_g'][l]), cb0[l], prm['ffn_w_up'][l],
                         prm['ffn_conv_w'][l], prm['ffn_conv_b'][l], prm['ffn_w_down'][l])
        x = x + f
        Cs.append(C); ns.append(n); ms.append(m); Ss.append(S); shs.append(sh); cbs.append(cb)
        if return_v:
            vs.append(vrows)
    y = rmsnorm(x, prm['final_norm_g'])
    st = [jnp.stack(a) for a in (Cs, ns, ms, Ss, shs, cbs)]
    if return_v:
        st.append(jnp.stack(vs))
    return y, st


def setup_inputs(seed: int = 0) -> dict:
    key = jax.random.key(seed)
    ks = iter(jax.random.split(key, 48))

    def nrm(shape, scale):
        return jax.random.normal(next(ks), shape, jnp.float32) * scale

    D = D_MODEL
    d = {}
    d['x_prompt'] = nrm((BATCH, SEQ, D), 1.0)
    d['x_sample'] = nrm((DEC_BATCH, DEC_SEQ, D), 1.0)
    d['state_mlstm_C'] = nrm((DEPTH, DEC_BATCH, ML_HEADS, ML_DIM, ML_DIM), 0.1)
    d['state_mlstm_n'] = jnp.abs(nrm((DEPTH, DEC_BATCH, ML_HEADS, ML_DIM), 0.5))
    d['state_mlstm_m'] = nrm((DEPTH, DEC_BATCH, ML_HEADS), 1.0)
    d['state_rwkv_S'] = nrm((DEPTH, DEC_BATCH, RW_HEADS, RW_HEAD, RW_HEAD), 0.1)
    d['state_rwkv_shift'] = nrm((DEPTH, DEC_BATCH, RW_COLS), 1.0)
    d['state_ffn_conv'] = nrm((DEPTH, DEC_BATCH, CONV_W - 1, 2 * D_FF), 1.0)
    d['mix_norm_g'] = 1.0 + nrm((DEPTH, D), 0.02)
    d['w_in'] = nrm((DEPTH, D, N_IN), D ** -0.5)
    d['mlstm_b_i'] = nrm((DEPTH, ML_HEADS), 0.1)
    d['mlstm_b_f'] = 3.0 + nrm((DEPTH, ML_HEADS), 0.5)
    d['mlstm_norm_g'] = 1.0 + nrm((DEPTH, ML_WIDTH), 0.02)
    d['rwkv_mu'] = jax.random.uniform(next(ks), (DEPTH, RW_COLS), jnp.float32)
    d['rwkv_w0'] = nrm((DEPTH, RW_WIDTH), 0.5)
    d['rwkv_w_up'] = nrm((DEPTH, RW_LORA_W, RW_WIDTH), RW_LORA_W ** -0.5)
    d['rwkv_a0'] = nrm((DEPTH, RW_WIDTH), 0.1)
    d['rwkv_a_up'] = nrm((DEPTH, RW_LORA_A, RW_WIDTH), RW_LORA_A ** -0.5)
    d['rwkv_g_up'] = nrm((DEPTH, RW_LORA_G, RW_WIDTH), RW_LORA_G ** -0.5)
    d['rwkv_k_k'] = 0.85 + nrm((DEPTH, RW_WIDTH), 0.02)
    d['rwkv_k_a'] = 1.0 + nrm((DEPTH, RW_WIDTH), 0.02)
    d['rwkv_r_k'] = nrm((DEPTH, RW_HEADS, RW_HEAD), 0.1)
    d['rwkv_ln_g'] = 1.0 + nrm((DEPTH, RW_WIDTH), 0.02)
    d['rwkv_ln_b'] = nrm((DEPTH, RW_WIDTH), 0.02)
    d['sgu_ln_g'] = 1.0 + nrm((DEPTH, SG_WIDTH), 0.02)
    d['sgu_ln_b'] = nrm((DEPTH, SG_WIDTH), 0.02)
    d['sgu_w'] = nrm((DEPTH, SG_GROUPS, SG_CHUNK, SG_CHUNK), SG_CHUNK ** -0.5)
    d['sgu_b'] = 1.0 + nrm((DEPTH, SG_GROUPS, SG_CHUNK), 0.02)
    d['w_out'] = nrm((DEPTH, D_MIX, D), D_MIX ** -0.5)
    d['ffn_norm_g'] = 1.0 + nrm((DEPTH, D), 0.02)
    d['ffn_w_up'] = nrm((DEPTH, D, 2 * D_FF), D ** -0.5)
    d['ffn_conv_w'] = nrm((DEPTH, CONV_W, 2 * D_FF), CONV_W ** -0.5)
    d['ffn_conv_b'] = nrm((DEPTH, 2 * D_FF), 0.02)
    d['ffn_w_down'] = nrm((DEPTH, D_FF, D), D_FF ** -0.5)
    d['final_norm_g'] = 1.0 + nrm((D,), 0.02)
    return d


def reference(x_prompt, x_sample, state_mlstm_C, state_mlstm_n, state_mlstm_m, state_rwkv_S,
              state_rwkv_shift, state_ffn_conv, mix_norm_g, w_in, mlstm_b_i, mlstm_b_f, mlstm_norm_g,
              rwkv_mu, rwkv_w0, rwkv_w_up, rwkv_a0, rwkv_a_up, rwkv_g_up, rwkv_k_k, rwkv_k_a, rwkv_r_k,
              rwkv_ln_g, rwkv_ln_b, sgu_ln_g, sgu_ln_b, sgu_w, sgu_b, w_out, ffn_norm_g, ffn_w_up,
              ffn_conv_w, ffn_conv_b, ffn_w_down, final_norm_g):
    prm = dict(mix_norm_g=mix_norm_g, w_in=w_in, mlstm_b_i=mlstm_b_i, mlstm_b_f=mlstm_b_f,
               mlstm_norm_g=mlstm_norm_g, rwkv_mu=rwkv_mu, rwkv_w0=rwkv_w0, rwkv_w_up=rwkv_w_up,
               rwkv_a0=rwkv_a0, rwkv_a_up=rwkv_a_up, rwkv_g_up=rwkv_g_up, rwkv_k_k=rwkv_k_k,
               rwkv_k_a=rwkv_k_a, rwkv_r_k=rwkv_r_k, rwkv_ln_g=rwkv_ln_g, rwkv_ln_b=rwkv_ln_b,
               sgu_ln_g=sgu_ln_g, sgu_ln_b=sgu_ln_b, sgu_w=sgu_w, sgu_b=sgu_b, w_out=w_out,
               ffn_norm_g=ffn_norm_g, ffn_w_up=ffn_w_up, ffn_conv_w=ffn_conv_w,
               ffn_conv_b=ffn_conv_b, ffn_w_down=ffn_w_down, final_norm_g=final_norm_g)
    bp = x_prompt.shape[0]
    y_prompt, st_p = trunk(
        x_prompt,
        jnp.zeros((DEPTH, bp, ML_HEADS, ML_DIM, ML_DIM), jnp.float32),
        jnp.zeros((DEPTH, bp, ML_HEADS, ML_DIM), jnp.float32),
        jnp.zeros((DEPTH, bp, ML_HEADS), jnp.float32),
        jnp.zeros((DEPTH, bp, RW_HEADS, RW_HEAD, RW_HEAD), jnp.float32),
        jnp.zeros((DEPTH, bp, RW_COLS), jnp.float32),
        jnp.zeros((DEPTH, bp, CONV_W - 1, 2 * D_FF), jnp.float32),
        prm, False)
    y_sample, st_s = trunk(x_sample, state_mlstm_C, state_mlstm_n, state_mlstm_m, state_rwkv_S,
                           state_rwkv_shift, state_ffn_conv, prm, True)
    C_p, n_p, m_p, S_p, sh_p, cb_p = st_p
    C_s, n_s, m_s, S_s, sh_s, cb_s, v_s = st_s
    return (y_prompt, y_sample, C_p, n_p, m_p, S_p, sh_p, cb_p, C_s, n_s, m_s, S_s, sh_s, cb_s, v_s)
```

```python
import functools

import jax
import jax.numpy as jnp
from jax import lax
from jax.experimental import pallas as pl
from jax.experimental.pallas import tpu as pltpu

F32 = jnp.float32
BF16 = jnp.bfloat16
HI = lax.Precision.HIGHEST

D_MODEL = 2048
DEPTH = 2
ML_HEADS = 4
ML_DIM = 128
ML_WIDTH = 512
ML_CHUNK = 64
RW_HEAD = 64
RW_WIDTH = 1024
RW_HEADS = 16
RW_COLS = 3 * RW_WIDTH + 64 + 64 + 128
ML_COLS = 4 * ML_WIDTH + 2 * ML_HEADS
SG_WIDTH = 512
SG_GROUPS = 4
SG_CHUNK = 128
D_FF = 5632
EPS = 1e-6
RW_GN_EPS = RW_HEAD * 1e-5

TN_IN = 512
ML_PACK = 2560
RW_PACK = 3584
SG_PACK = 1024
N_PACK = ML_PACK + RW_PACK + SG_PACK
ML_TILES = ML_PACK // TN_IN
RW_TILES = RW_PACK // TN_IN
SG_TILES = SG_PACK // TN_IN

RW_CHUNK = 64
SAMPLE_T_PAD = 8
TF_FFN = 512
VMEM_LIMIT = 56 * 1024 * 1024


def _cparams(sem):
    return pltpu.CompilerParams(dimension_semantics=sem, vmem_limit_bytes=VMEM_LIMIT)


def _dot(a, b):
    return jnp.dot(a, b, preferred_element_type=F32)


def _dot_hi(a, b):
    return jnp.dot(a, b, precision=HI, preferred_element_type=F32)


def _dot_nt_hi(a, b):
    return lax.dot_general(a, b, (((1,), (1,)), ((), ())), precision=HI, preferred_element_type=F32)


def _dot_tn_hi(a, b):
    return lax.dot_general(a, b, (((0,), (0,)), ((), ())), precision=HI, preferred_element_type=F32)


def _rms(x, g):
    return x * lax.rsqrt(jnp.mean(x * x, axis=-1, keepdims=True) + EPS) * g


def _sigmoid(x):
    return 1.0 / (1.0 + jnp.exp(-x))


def _softplus(x):
    return jnp.maximum(x, 0.0) + jnp.log1p(jnp.exp(-jnp.abs(x)))


def _in_proj_kernel(x_ref, g_ref, w_ref, ml_ref, rw_ref, sg_ref, xn_ref):
    j = pl.program_id(1)

    @pl.when(j == 0)
    def _():
        xn_ref[...] = _rms(x_ref[...], g_ref[...]).astype(BF16)

    res = _dot(xn_ref[...], w_ref[...])

    @pl.when(j < ML_TILES)
    def _():
        ml_ref[...] = res

    @pl.when(jnp.logical_and(j >= ML_TILES, j < ML_TILES + RW_TILES))
    def _():
        rw_ref[...] = res

    @pl.when(j >= ML_TILES + RW_TILES)
    def _():
        sg_ref[...] = res


def _in_proj(x, g, w_packed, tm):
    m = x.shape[0]
    n_tiles = N_PACK // TN_IN
    return pl.pallas_call(
        _in_proj_kernel,
        grid=(m // tm, n_tiles),
        in_specs=[
            pl.BlockSpec((tm, D_MODEL), lambda i, j: (i, 0)),
            pl.BlockSpec((1, D_MODEL), lambda i, j: (0, 0)),
            pl.BlockSpec((D_MODEL, TN_IN), lambda i, j: (0, j)),
        ],
        out_specs=[
            pl.BlockSpec((tm, TN_IN), lambda i, j: (i, jnp.minimum(j, ML_TILES - 1))),
            pl.BlockSpec((tm, TN_IN), lambda i, j: (i, jnp.clip(j - ML_TILES, 0, RW_TILES - 1))),
            pl.BlockSpec((tm, TN_IN), lambda i, j: (i, jnp.clip(j - ML_TILES - RW_TILES, 0, SG_TILES - 1))),
        ],
        out_shape=[
            jax.ShapeDtypeStruct((m, ML_PACK), F32),
            jax.ShapeDtypeStruct((m, RW_PACK), F32),
            jax.ShapeDtypeStruct((m, SG_PACK), F32),
        ],
        scratch_shapes=[pltpu.VMEM((tm, D_MODEL), BF16)],
        compiler_params=_cparams(("parallel", "arbitrary")),
        name="in_proj",
    )(x, g, w_packed)


def _mlstm_kernel(q_ref, k_ref, v_ref, o_ref, gc_ref, gr_ref, bic_ref, bir_ref, ng_ref,
                  c0_ref, n0_ref, m0_ref,
                  y_ref, cout_ref, nout_ref, mout_ref,
                  c_scr, n_scr, m_scr, *, chunk, t_valid):
    L = chunk
    c_idx = pl.program_id(1)

    @pl.when(c_idx == 0)
    def _():
        c_scr[...] = c0_ref[0]
        n_scr[...] = n0_ref[0]
        m_scr[...] = m0_ref[0]

    row = lax.broadcasted_iota(jnp.int32, (L, L), 0)
    col = lax.broadcasted_iota(jnp.int32, (L, L), 1)
    causal = row >= col
    gcol = gc_ref[0] + bir_ref[...]
    grow = gr_ref[0, 0] + bic_ref[...]
    rvalid = lax.broadcasted_iota(jnp.int32, (L, 1), 0) < t_valid
    cvalid = lax.broadcasted_iota(jnp.int32, (1, L), 1) < t_valid
    scale = ML_DIM ** -0.5

    for h in range(ML_HEADS):
        sl = slice(h * ML_DIM, (h + 1) * ML_DIM)
        q = q_ref[0, :, sl]
        k = k_ref[0, :, sl] * scale
        v = v_ref[0, :, sl]
        ig_c = gcol[:, h:h + 1]
        f_c = gcol[:, ML_HEADS + h:ML_HEADS + h + 1]
        ig_r = grow[h:h + 1, :]
        f_r = grow[ML_HEADS + h:ML_HEADS + h + 1, :]
        lf_c = -_softplus(-f_c)
        lf_r = -_softplus(-f_r)
        if t_valid < L:
            ig_c = jnp.where(rvalid, ig_c, -jnp.inf)
            ig_r = jnp.where(cvalid, ig_r, -jnp.inf)
            lf_c = jnp.where(rvalid, lf_c, 0.0)
            lf_r = jnp.where(cvalid, lf_r, 0.0)
        b_c = jnp.sum(jnp.where(causal, lf_r, 0.0), axis=-1, keepdims=True)
        b_r = jnp.sum(jnp.where(row <= col, lf_c, 0.0), axis=0, keepdims=True)
        b_last = jnp.sum(lf_r, axis=-1, keepdims=True)
        m_prev = m_scr[h][:, 0:1]
        C = c_scr[h]
        n_row = n_scr[h]

        inter = b_c + m_prev
        dmat = jnp.where(causal, b_c - b_r + ig_r, -jnp.inf)
        m_row = jnp.maximum(inter, jnp.max(dmat, axis=-1, keepdims=True))
        s_inter = jnp.exp(inter - m_row)
        qb = q.astype(BF16)
        kb = k.astype(BF16)
        vb = v.astype(BF16)
        qk = lax.dot_general(qb, kb, (((1,), (1,)), ((), ())), preferred_element_type=F32)
        att = qk * jnp.exp(dmat - m_row)
        num = s_inter * _dot(qb, C.astype(BF16)) + _dot(att.astype(BF16), vb)
        den = s_inter * jnp.sum(q * n_row, axis=-1, keepdims=True) + jnp.sum(att, axis=-1, keepdims=True)
        hh = num / jnp.maximum(jnp.abs(den), jnp.exp(-m_row))

        g_c = b_last - b_c + ig_c
        m_new = jnp.maximum(b_last + m_prev, jnp.max(g_c, axis=0, keepdims=True))
        wk_c = jnp.exp(g_c - m_new)
        dec = jnp.exp(b_last + m_prev - m_new)
        kw = k * wk_c
        c_scr[h] = dec * C + lax.dot_general(kw.astype(BF16), vb, (((0,), (0,)), ((), ())),
                                             preferred_element_type=F32)
        n_scr[h] = dec * n_row + jnp.sum(kw, axis=0, keepdims=True)
        m_scr[h] = jnp.broadcast_to(m_new, (1, ML_DIM))

        hn = hh * lax.rsqrt(jnp.mean(hh * hh, axis=-1, keepdims=True) + EPS)
        y_ref[0, :, sl] = hn * ng_ref[:, sl] * _sigmoid(o_ref[0, :, sl])

    cout_ref[0] = c_scr[...]
    nout_ref[0] = n_scr[...]
    mout_ref[0] = m_scr[...]


def _mlstm(p_ml3, g_rows, b_col, b_row, norm_g, c0, n0, m0, chunk, t_valid):
    bg, tp, _ = p_ml3.shape
    nc = tp // chunk
    L = chunk
    kern = functools.partial(_mlstm_kernel, chunk=chunk, t_valid=t_valid)
    st4 = lambda b, c: (b, 0, 0, 0)
    return pl.pallas_call(
        kern,
        grid=(bg, nc),
        in_specs=[
            pl.BlockSpec((1, L, ML_WIDTH), lambda b, c: (b, c, 0)),
            pl.BlockSpec((1, L, ML_WIDTH), lambda b, c: (b, c, 1)),
            pl.BlockSpec((1, L, ML_WIDTH), lambda b, c: (b, c, 2)),
            pl.BlockSpec((1, L, ML_WIDTH), lambda b, c: (b, c, 3)),
            pl.BlockSpec((1, L, ML_WIDTH), lambda b, c: (b, c, 4)),
            pl.BlockSpec((1, 1, 2 * ML_HEADS, L), lambda b, c: (b, c, 0, 0)),
            pl.BlockSpec((2 * ML_HEADS, 1), lambda b, c: (0, 0)),
            pl.BlockSpec((1, ML_WIDTH), lambda b, c: (0, 0)),
            pl.BlockSpec((1, ML_WIDTH), lambda b, c: (0, 0)),
            pl.BlockSpec((1, ML_HEADS, ML_DIM, ML_DIM), st4),
            pl.BlockSpec((1, ML_HEADS, 1, ML_DIM), st4),
            pl.BlockSpec((1, ML_HEADS, 1, ML_DIM), st4),
        ],
        out_specs=[
            pl.BlockSpec((1, L, ML_WIDTH), lambda b, c: (b, c, 0)),
            pl.BlockSpec((1, ML_HEADS, ML_DIM, ML_DIM), st4),
            pl.BlockSpec((1, ML_HEADS, 1, ML_DIM), st4),
            pl.BlockSpec((1, ML_HEADS, 1, ML_DIM), st4),
        ],
        out_shape=[
            jax.ShapeDtypeStruct((bg, tp, ML_WIDTH), F32),
            jax.ShapeDtypeStruct((bg, ML_HEADS, ML_DIM, ML_DIM), F32),
            jax.ShapeDtypeStruct((bg, ML_HEADS, 1, ML_DIM), F32),
            jax.ShapeDtypeStruct((bg, ML_HEADS, 1, ML_DIM), F32),
        ],
        scratch_shapes=[
            pltpu.VMEM((ML_HEADS, ML_DIM, ML_DIM), F32),
            pltpu.VMEM((ML_HEADS, 1, ML_DIM), F32),
            pltpu.VMEM((ML_HEADS, 1, ML_DIM), F32),
        ],
        compiler_params=_cparams(("parallel", "arbitrary")),
        name="mlstm",
    )(p_ml3, p_ml3, p_ml3, p_ml3, p_ml3, g_rows, b_col, b_row, norm_g, c0, n0, m0)


def _tri_inverse(nmat, L, row, col):
    eye = (row == col).astype(F32)
    d = jnp.where((row >> 3) == (col >> 3), nmat, 0.0)
    d2 = _dot_hi(d, d)
    d4 = _dot_hi(d2, d2)
    p1 = eye + d + d2 + _dot_hi(d, d2)
    t = p1 + _dot_hi(p1, d4)
    s = 8
    while s < L:
        sh = s.bit_length() - 1
        pair = (row >> (sh + 1)) == (col >> (sh + 1))
        low_left = jnp.logical_and(((row >> sh) & 1) == 1, ((col >> sh) & 1) == 0)
        e = jnp.where(jnp.logical_and(pair, low_left), nmat, 0.0)
        t = t + _dot_hi(_dot_hi(t, e), t)
        s *= 2
    return t


def _rwkv_kernel(p_ref, sh0_ref, s0_ref, mu_ref, w0_ref, wup_ref, a0_ref, aup_ref, gup_ref,
                 kk_ref, ka_ref, rk_ref, lng_ref, lnb_ref,
                 y_ref, sout_ref,
                 s_scr, carry_scr, r_s, kk_s, km_s, v_s, lw_s, al_s, rk_s, yn_s, bo_s,
                 *, chunk, t_valid):
    L = chunk
    c_idx = pl.program_id(1)

    @pl.when(c_idx == 0)
    def _():
        s_scr[...] = s0_ref[0]
        carry_scr[0:1, :] = sh0_ref[0]

    p = p_ref[0]
    rowp = lax.broadcasted_iota(jnp.int32, (L, 1), 0)
    prev = jnp.where(rowp == 0, carry_scr[0:1, :], pltpu.roll(p, 1, 0))
    carry_scr[0:1, :] = p[L - 1:L, :]
    px = p + (prev - p) * mu_ref[...]
    W = RW_WIDTH
    r = px[:, 0:W]
    k = px[:, W:2 * W]
    v = px[:, 2 * W:3 * W]
    xw = px[:, 3 * W:3 * W + 64]
    xa = px[:, 3 * W + 64:3 * W + 128]
    xg = px[:, 3 * W + 128:3 * W + 256]
    w_log = -_softplus(-(w0_ref[...] + _dot(jnp.tanh(xw).astype(BF16), wup_ref[...]))) - 0.5
    logw = -jnp.exp(w_log)
    alr = _sigmoid(a0_ref[...] + _dot(xa.astype(BF16), aup_ref[...]))
    gate = _dot(_sigmoid(xg).astype(BF16), gup_ref[...])
    kk_raw = k * kk_ref[...]
    k_mod = k * (1.0 + (alr - 1.0) * ka_ref[...])
    if t_valid < L:
        valid = rowp < t_valid
        logw = jnp.where(valid, logw, 0.0)
        kk_raw = jnp.where(valid, kk_raw, 0.0)
        k_mod = jnp.where(valid, k_mod, 0.0)
        v = jnp.where(valid, v, 0.0)
        alr = jnp.where(valid, alr, 0.0)
    rkp = r * k_mod * rk_ref[...]
    for h in range(RW_HEADS):
        sl = slice(h * RW_HEAD, (h + 1) * RW_HEAD)
        r_s[h] = r[:, sl]
        kk_s[h] = kk_raw[:, sl]
        km_s[h] = k_mod[:, sl]
        v_s[h] = v[:, sl]
        lw_s[h] = logw[:, sl]
        al_s[h] = alr[:, sl]
        rk_s[h] = rkp[:, sl]

    row = lax.broadcasted_iota(jnp.int32, (L, L), 0)
    col = lax.broadcasted_iota(jnp.int32, (L, L), 1)
    tri_incl = (row >= col).astype(F32)
    lower_strict = row > col
    lower_incl = row >= col

    def head(h, carry):
        rh = r_s[h]
        kkh = kk_s[h]
        kmh = km_s[h]
        vh = v_s[h]
        lwh = lw_s[h]
        alh = al_s[h]
        kkh = kkh * lax.rsqrt(jnp.maximum(jnp.sum(kkh * kkh, axis=-1, keepdims=True), 1e-24))
        ah = -kkh
        bh = kkh * alh
        c_incl = _dot_hi(tri_incl, lwh)
        c_excl = c_incl - lwh
        e_incl = jnp.exp(c_incl)
        e_neg = jnp.exp(-c_incl)
        at = ah * jnp.exp(c_excl)
        rt = rh * e_incl
        bt = bh * e_neg
        kt = kmh * e_neg
        g = _dot_nt_hi(jnp.concatenate([at, rt], axis=0), jnp.concatenate([bt, kt], axis=0))
        a_ab = jnp.where(lower_strict, g[:L, :L], 0.0)
        a_ak = jnp.where(lower_strict, g[:L, L:], 0.0)
        q_b = jnp.where(lower_incl, g[L:, :L], 0.0)
        q_k = jnp.where(lower_incl, g[L:, L:], 0.0)
        tinv = _tri_inverse(a_ab, L, row, col)
        S = s_scr[h]
        u = _dot_hi(tinv, _dot_nt_hi(at, S) + _dot_hi(a_ak, vh))
        y = _dot_nt_hi(rt, S) + _dot_hi(q_b, u) + _dot_hi(q_k, vh)
        e_last = e_incl[L - 1:L, :]
        s_scr[h] = S * e_last + _dot_tn_hi(u, bt * e_last) + _dot_tn_hi(vh, kt * e_last)
        mean = jnp.mean(y, axis=-1, keepdims=True)
        yc = y - mean
        var = jnp.mean(yc * yc, axis=-1, keepdims=True)
        yn_s[h] = yc * lax.rsqrt(var + RW_GN_EPS)
        bo_s[h] = jnp.sum(rk_s[h], axis=-1, keepdims=True) * vh
        return carry

    lax.fori_loop(0, RW_HEADS, head, 0)

    for h in range(RW_HEADS):
        sl = slice(h * RW_HEAD, (h + 1) * RW_HEAD)
        y_ref[0, :, sl] = (yn_s[h] * lng_ref[:, sl] + lnb_ref[:, sl] + bo_s[h]) * gate[:, sl]
    sout_ref[0] = s_scr[...]


def _rwkv(p_rw3, sh0, s0, prm, chunk, t_valid):
    bg, tp, _ = p_rw3.shape
    nc = tp // chunk
    L = chunk
    kern = functools.partial(_rwkv_kernel, chunk=chunk, t_valid=t_valid)
    full = lambda shape: pl.BlockSpec(shape, lambda b, c: tuple(0 for _ in shape))
    head_scr = pltpu.VMEM((RW_HEADS, L, RW_HEAD), F32)
    return pl.pallas_call(
        kern,
        grid=(bg, nc),
        in_specs=[
            pl.BlockSpec((1, L, RW_PACK), lambda b, c: (b, c, 0)),
            pl.BlockSpec((1, 1, RW_PACK), lambda b, c: (b, 0, 0)),
            pl.BlockSpec((1, RW_HEADS, RW_HEAD, RW_HEAD), lambda b, c: (b, 0, 0, 0)),
            full((1, RW_PACK)),
            full((1, RW_WIDTH)), full((64, RW_WIDTH)),
            full((1, RW_WIDTH)), full((64, RW_WIDTH)),
            full((128, RW_WIDTH)),
            full((1, RW_WIDTH)), full((1, RW_WIDTH)), full((1, RW_WIDTH)),
            full((1, RW_WIDTH)), full((1, RW_WIDTH)),
        ],
        out_specs=[
            pl.BlockSpec((1, L, RW_WIDTH), lambda b, c: (b, c, 0)),
            pl.BlockSpec((1, RW_HEADS, RW_HEAD, RW_HEAD), lambda b, c: (b, 0, 0, 0)),
        ],
        out_shape=[
            jax.ShapeDtypeStruct((bg, tp, RW_WIDTH), F32),
            jax.ShapeDtypeStruct((bg, RW_HEADS, RW_HEAD, RW_HEAD), F32),
        ],
        scratch_shapes=[
            pltpu.VMEM((RW_HEADS, RW_HEAD, RW_HEAD), F32),
            pltpu.VMEM((8, RW_PACK), F32),
        ] + [head_scr] * 9,
        compiler_params=_cparams(("parallel", "arbitrary")),
        name="rwkv7",
    )(p_rw3, sh0, s0, prm["mu"], prm["w0"], prm["w_up"], prm["a0"], prm["a_up"], prm["g_up"],
      prm["k_k"], prm["k_a"], prm["r_k"], prm["ln_g"], prm["ln_b"])


def _sgu_kernel(p_ref, lng_ref, lnb_ref, w_ref, bs_ref, y_ref, v_ref):
    x = p_ref[...]
    z = 0.5 * x * (1.0 + lax.erf(x * (2.0 ** -0.5)))
    row = lax.broadcasted_iota(jnp.int32, (SG_CHUNK, SG_CHUNK), 0)
    col = lax.broadcasted_iota(jnp.int32, (SG_CHUNK, SG_CHUNK), 1)
    causal = row >= col
    for g in range(SG_GROUPS):
        sl = slice(g * 128, (g + 1) * 128)
        u = z[:, sl]
        vg = z[:, SG_WIDTH + g * 128:SG_WIDTH + (g + 1) * 128]
        mean = jnp.mean(vg, axis=-1, keepdims=True)
        vc = vg - mean
        var = jnp.mean(vc * vc, axis=-1, keepdims=True)
        vn = vc * lax.rsqrt(var + EPS) * lng_ref[:, sl] + lnb_ref[:, sl]
        wg = jnp.where(causal, w_ref[g], 0.0).astype(BF16)
        mixed = _dot(wg, vn.astype(BF16)) + bs_ref[:, g:g + 1]
        y_ref[:, sl] = u * mixed
        v_ref[:, sl] = vn


def _sgu(p_sg, ln_g, ln_b, w_mix, bs_col):
    m = p_sg.shape[0]
    return pl.pallas_call(
        _sgu_kernel,
        grid=(m // SG_CHUNK,),
        in_specs=[
            pl.BlockSpec((SG_CHUNK, SG_PACK), lambda i: (i, 0)),
            pl.BlockSpec((1, SG_WIDTH), lambda i: (0, 0)),
            pl.BlockSpec((1, SG_WIDTH), lambda i: (0, 0)),
            pl.BlockSpec((SG_GROUPS, SG_CHUNK, SG_CHUNK), lambda i: (0, 0, 0)),
            pl.BlockSpec((SG_CHUNK, SG_GROUPS), lambda i: (0, 0)),
        ],
        out_specs=[
            pl.BlockSpec((SG_CHUNK, SG_WIDTH), lambda i: (i, 0)),
            pl.BlockSpec((SG_CHUNK, SG_WIDTH), lambda i: (i, 0)),
        ],
        out_shape=[
            jax.ShapeDtypeStruct((m, SG_WIDTH), F32),
            jax.ShapeDtypeStruct((m, SG_WIDTH), F32),
        ],
        compiler_params=_cparams(("parallel",)),
        name="sgu",
    )(p_sg, ln_g, ln_b, w_mix, bs_col)


def _out_proj_kernel(x_ref, ym_ref, yr_ref, ys_ref, wm_ref, wr_ref, ws_ref, o_ref):
    acc = _dot(ym_ref[...].astype(BF16), wm_ref[...])
    acc = acc + _dot(yr_ref[...].astype(BF16), wr_ref[...])
    acc = acc + _dot(ys_ref[...].astype(BF16), ws_ref[...])
    o_ref[...] = x_ref[...] + acc


def _out_proj(x, ym, yr, ys, w_m, w_r, w_s, tm):
    m = x.shape[0]
    return pl.pallas_call(
        _out_proj_kernel,
        grid=(m // tm,),
        in_specs=[
            pl.BlockSpec((tm, D_MODEL), lambda i: (i, 0)),
            pl.BlockSpec((tm, ML_WIDTH), lambda i: (i, 0)),
            pl.BlockSpec((tm, RW_WIDTH), lambda i: (i, 0)),
            pl.BlockSpec((tm, SG_WIDTH), lambda i: (i, 0)),
            pl.BlockSpec((ML_WIDTH, D_MODEL), lambda i: (0, 0)),
            pl.BlockSpec((RW_WIDTH, D_MODEL), lambda i: (0, 0)),
            pl.BlockSpec((SG_WIDTH, D_MODEL), lambda i: (0, 0)),
        ],
        out_specs=pl.BlockSpec((tm, D_MODEL), lambda i: (i, 0)),
        out_shape=jax.ShapeDtypeStruct((m, D_MODEL), F32),
        compiler_params=_cparams(("parallel",)),
        name="out_proj",
    )(x, ym, yr, ys, w_m, w_r, w_s)


def _ffn_kernel(*refs, tm, n_f, seq_tiles, sample, final):
    if sample:
        (x_ref, g_ref, wg_ref, wv_ref, cwg_ref, cwv_ref, cbg_ref, cbv_ref, wd_ref, fg_ref,
         e1g_ref, e1v_ref, e2g_ref, e2v_ref,
         o_ref, hg_ref, hv_ref, xn_ref, acc_ref, cg_scr, cv_scr) = refs
    else:
        (x_ref, g_ref, wg_ref, wv_ref, cwg_ref, cwv_ref, cbg_ref, cbv_ref, wd_ref, fg_ref,
         o_ref, hg_ref, hv_ref, xn_ref, acc_ref, cg_scr, cv_scr) = refs
    i = pl.program_id(0)
    f = pl.program_id(1)

    @pl.when(f == 0)
    def _():
        xn_ref[...] = _rms(x_ref[...], g_ref[...]).astype(BF16)
        acc_ref[...] = jnp.zeros_like(acc_ref)

    xn = xn_ref[...]
    hg = _dot(xn, wg_ref[...])
    hv = _dot(xn, wv_ref[...])
    row = lax.broadcasted_iota(jnp.int32, (tm, 1), 0)

    def conv(h, cw_ref, cb_ref, carry_scr, e1_ref, e2_ref):
        r1 = pltpu.roll(h, 1, 0)
        r2 = pltpu.roll(h, 2, 0)
        if sample:
            t = row & 3
            hm1 = jnp.where(t >= 1, r1, e1_ref[...])
            hm2 = jnp.where(t >= 2, r2, e2_ref[...])
        else:
            first = (i % seq_tiles) == 0
            c = carry_scr[f]
            cm2 = jnp.where(first, 0.0, c[0:1, :])
            cm1 = jnp.where(first, 0.0, c[1:2, :])
            hm1 = jnp.where(row == 0, cm1, r1)
            hm2 = jnp.where(row == 0, cm2, jnp.where(row == 1, cm1, r2))
            carry_scr[f, 0:2, :] = h[tm - 2:tm, :]
        return cb_ref[...] + cw_ref[2:3, :] * h + cw_ref[1:2, :] * hm1 + cw_ref[0:1, :] * hm2

    if sample:
        cg = conv(hg, cwg_ref, cbg_ref, cg_scr, e1g_ref, e2g_ref)
        cv = conv(hv, cwv_ref, cbv_ref, cv_scr, e1v_ref, e2v_ref)
        hg_ref[...] = hg
        hv_ref[...] = hv
    else:
        cg = conv(hg, cwg_ref, cbg_ref, cg_scr, None, None)
        cv = conv(hv, cwv_ref, cbv_ref, cv_scr, None, None)
        hg_ref[0] = hg[tm - 2:tm, :]
        hv_ref[0] = hv[tm - 2:tm, :]
    act = (cg * _sigmoid(cg) * cv).astype(BF16)
    acc_ref[...] += _dot(act, wd_ref[...])

    @pl.when(f == n_f - 1)
    def _():
        out = x_ref[...] + acc_ref[...]
        if final:
            out = _rms(out, fg_ref[...])
        o_ref[...] = out


def _ffn(x, g, w_up_bf, conv_w, conv_b, w_down_bf, final_g, *, tm, seq_rows, sample, final, e1=None, e2=None):
    m = x.shape[0]
    tf = TF_FFN
    n_f = D_FF // tf
    n_m = m // tm
    seq_tiles = max(seq_rows // tm, 1)
    bg = m // seq_rows
    kern = functools.partial(_ffn_kernel, tm=tm, n_f=n_f, seq_tiles=seq_tiles, sample=sample, final=final)
    in_specs = [
        pl.BlockSpec((tm, D_MODEL), lambda i, f: (i, 0)),
        pl.BlockSpec((1, D_MODEL), lambda i, f: (0, 0)),
        pl.BlockSpec((D_MODEL, tf), lambda i, f: (0, f)),
        pl.BlockSpec((D_MODEL, tf), lambda i, f: (0, n_f + f)),
        pl.BlockSpec((3, tf), lambda i, f: (0, f)),
        pl.BlockSpec((3, tf), lambda i, f: (0, n_f + f)),
        pl.BlockSpec((1, tf), lambda i, f: (0, f)),
        pl.BlockSpec((1, tf), lambda i, f: (0, n_f + f)),
        pl.BlockSpec((tf, D_MODEL), lambda i, f: (f, 0)),
        pl.BlockSpec((1, D_MODEL), lambda i, f: (0, 0)),
    ]
    args = [x, g, w_up_bf, w_up_bf, conv_w, conv_w, conv_b, conv_b, w_down_bf, final_g]
    if sample:
        in_specs += [
            pl.BlockSpec((tm, tf), lambda i, f: (i, f)),
            pl.BlockSpec((tm, tf), lambda i, f: (i, n_f + f)),
            pl.BlockSpec((tm, tf), lambda i, f: (i, f)),
            pl.BlockSpec((tm, tf), lambda i, f: (i, n_f + f)),
        ]
        args += [e1, e1, e2, e2]
        h_specs = [pl.BlockSpec((tm, tf), lambda i, f: (i, f))] * 2
        h_shapes = [jax.ShapeDtypeStruct((m, D_FF), F32)] * 2
    else:
        h_specs = [pl.BlockSpec((1, 2, tf), lambda i, f: (i, 0, f))] * 2
        h_shapes = [jax.ShapeDtypeStruct((n_m, 2, D_FF), F32)] * 2
    return pl.pallas_call(
        kern,
        grid=(n_m, n_f),
        in_specs=in_specs,
        out_specs=[pl.BlockSpec((tm, D_MODEL), lambda i, f: (i, 0))] + h_specs,
        out_shape=[jax.ShapeDtypeStruct((m, D_MODEL), F32)] + h_shapes,
        scratch_shapes=[
            pltpu.VMEM((tm, D_MODEL), BF16),
            pltpu.VMEM((tm, D_MODEL), F32),
            pltpu.VMEM((n_f, 8, tf), F32),
            pltpu.VMEM((n_f, 8, tf), F32),
        ],
        compiler_params=_cparams(("arbitrary", "arbitrary")),
        name="conv_ffn",
    )(*args)


def _pack_w_in(w):
    ml = jnp.pad(w[:, :ML_COLS], ((0, 0), (0, ML_PACK - ML_COLS)))
    rw = jnp.pad(w[:, ML_COLS:ML_COLS + RW_COLS], ((0, 0), (0, RW_PACK - RW_COLS)))
    sg = w[:, ML_COLS + RW_COLS:]
    return jnp.concatenate([ml, rw, sg], axis=1).astype(BF16)


def _trunk(x3, states, lw, final_g, *, sample):
    bg, t, _ = x3.shape
    m = bg * t
    tp = SAMPLE_T_PAD if sample else t
    ml_chunk = SAMPLE_T_PAD if sample else ML_CHUNK
    rw_chunk = SAMPLE_T_PAD if sample else RW_CHUNK
    x = x3.reshape(m, D_MODEL)
    outs = dict(C=[], n=[], m=[], S=[], sh=[], cb=[], v=[])
    for l in range(DEPTH):
        w = lw[l]
        st = states[l]
        p_ml, p_rw, p_sg = _in_proj(x, w["mix_g"], w["w_in"], 512)

        def seq(a):
            a = a.reshape(bg, t, a.shape[-1])
            if sample:
                a = jnp.pad(a, ((0, 0), (0, tp - t), (0, 0)))
            return a

        p_ml3, p_rw3, p_sg3 = seq(p_ml), seq(p_rw), seq(p_sg)
        nc_ml = tp // ml_chunk
        g_rows = p_ml3[:, :, 4 * ML_WIDTH:4 * ML_WIDTH + 2 * ML_HEADS]
        g_rows = g_rows.reshape(bg, nc_ml, ml_chunk, 2 * ML_HEADS).transpose(0, 1, 3, 2)
        ym, c_new, n_new, m_new = _mlstm(p_ml3, g_rows, w["ml_b_col"], w["ml_b_row"], w["ml_norm_g"],
                                         st["C"], st["n"], st["m"], ml_chunk, t)
        yr, s_new = _rwkv(p_rw3, st["sh"], st["S"], w, rw_chunk, t)
        ys, vrows = _sgu(p_sg3.reshape(bg * tp, SG_PACK), w["sg_ln_g"], w["sg_ln_b"],
                         w["sg_w_s"] if sample else w["sg_w_p"], w["sg_b_s"] if sample else w["sg_b_p"])

        def unseq(a):
            return a.reshape(bg, tp, a.shape[-1])[:, :t].reshape(m, a.shape[-1])

        x = _out_proj(x, unseq(ym), unseq(yr), unseq(ys), w["w_out_m"], w["w_out_r"], w["w_out_s"], 512)
        final = l == DEPTH - 1
        if sample:
            buf = st["cb"]
            z = jnp.zeros((bg, t, 2 * D_FF), F32)
            e1 = z.at[:, 0].set(buf[:, 1]).reshape(m, 2 * D_FF)
            e2 = z.at[:, 0].set(buf[:, 0]).at[:, 1].set(buf[:, 1]).reshape(m, 2 * D_FF)
            x, hg, hv = _ffn(x, w["ffn_g"], w["ffn_w_up"], w["conv_w"], w["conv_b"], w["w_down"], final_g,
                             tm=512, seq_rows=t, sample=True, final=final, e1=e1, e2=e2)
            hfull = jnp.concatenate([hg, hv], axis=-1).reshape(bg, t, 2 * D_FF)
            cb = hfull[:, t - 2:]
        else:
            x, hg, hv = _ffn(x, w["ffn_g"], w["ffn_w_up"], w["conv_w"], w["conv_b"], w["w_down"], final_g,
                             tm=512, seq_rows=t, sample=False, final=final)
            tiles = t // 512
            cb = jnp.concatenate([hg, hv], axis=-1)[tiles - 1::tiles]
        outs["C"].append(c_new)
        outs["n"].append(n_new[:, :, 0, :])
        outs["m"].append(m_new[:, :, 0, 0])
        outs["S"].append(s_new)
        outs["sh"].append(p_rw.reshape(bg, t, RW_PACK)[:, t - 1, :RW_COLS])
        outs["cb"].append(cb)
        outs["v"].append(vrows.reshape(bg, tp, SG_WIDTH)[:, :t])
    y = x.reshape(bg, t, D_MODEL)
    return y, {k: jnp.stack(v) for k, v in outs.items()}


def kernel(x_prompt, x_sample, state_mlstm_C, state_mlstm_n, state_mlstm_m, state_rwkv_S, state_rwkv_shift, state_ffn_conv, mix_norm_g, w_in, mlstm_b_i, mlstm_b_f, mlstm_norm_g, rwkv_mu, rwkv_w0, rwkv_w_up, rwkv_a0, rwkv_a_up, rwkv_g_up, rwkv_k_k, rwkv_k_a, rwkv_r_k, rwkv_ln_g, rwkv_ln_b, sgu_ln_g, sgu_ln_b, sgu_w, sgu_b, w_out, ffn_norm_g, ffn_w_up, ffn_conv_w, ffn_conv_b, ffn_w_down, final_norm_g):
    bp = x_prompt.shape[0]
    bs = x_sample.shape[0]
    lw = []
    reps = SG_CHUNK // SAMPLE_T_PAD
    for l in range(DEPTH):
        b_all = jnp.concatenate([mlstm_b_i[l], mlstm_b_f[l]])
        w8 = sgu_w[l][:, :SAMPLE_T_PAD, :SAMPLE_T_PAD]
        lw.append(dict(
            mix_g=mix_norm_g[l][None, :],
            w_in=_pack_w_in(w_in[l]),
            ml_b_col=b_all[:, None],
            ml_b_row=jnp.pad(b_all, (0, ML_WIDTH - 2 * ML_HEADS))[None, :],
            ml_norm_g=mlstm_norm_g[l][None, :],
            mu=jnp.pad(rwkv_mu[l], (0, RW_PACK - RW_COLS))[None, :],
            w0=rwkv_w0[l][None, :], w_up=rwkv_w_up[l].astype(BF16),
            a0=rwkv_a0[l][None, :], a_up=rwkv_a_up[l].astype(BF16),
            g_up=rwkv_g_up[l].astype(BF16),
            k_k=rwkv_k_k[l][None, :], k_a=rwkv_k_a[l][None, :],
            r_k=rwkv_r_k[l].reshape(1, RW_WIDTH),
            ln_g=rwkv_ln_g[l][None, :], ln_b=rwkv_ln_b[l][None, :],
            sg_ln_g=sgu_ln_g[l][None, :], sg_ln_b=sgu_ln_b[l][None, :],
            sg_w_p=sgu_w[l], sg_b_p=sgu_b[l].T,
            sg_w_s=jax.vmap(lambda a: jnp.kron(jnp.eye(reps, dtype=F32), a))(w8),
            sg_b_s=jnp.tile(sgu_b[l][:, :SAMPLE_T_PAD].T, (reps, 1)),
            w_out_m=w_out[l][:ML_WIDTH].astype(BF16),
            w_out_r=w_out[l][ML_WIDTH:ML_WIDTH + RW_WIDTH].astype(BF16),
            w_out_s=w_out[l][ML_WIDTH + RW_WIDTH:].astype(BF16),
            ffn_g=ffn_norm_g[l][None, :],
            ffn_w_up=ffn_w_up[l].astype(BF16),
            conv_w=ffn_conv_w[l], conv_b=ffn_conv_b[l][None, :],
            w_down=ffn_w_down[l].astype(BF16),
        ))
    fg = final_norm_g[None, :]

    zero_states = [dict(
        C=jnp.zeros((bp, ML_HEADS, ML_DIM, ML_DIM), F32),
        n=jnp.zeros((bp, ML_HEADS, 1, ML_DIM), F32),
        m=jnp.zeros((bp, ML_HEADS, 1, ML_DIM), F32),
        S=jnp.zeros((bp, RW_HEADS, RW_HEAD, RW_HEAD), F32),
        sh=jnp.zeros((bp, 1, RW_PACK), F32),
        cb=None,
    ) for _ in range(DEPTH)]
    samp_states = [dict(
        C=state_mlstm_C[l],
        n=state_mlstm_n[l][:, :, None, :],
        m=jnp.broadcast_to(state_mlstm_m[l][:, :, None, None], (bs, ML_HEADS, 1, ML_DIM)),
        S=state_rwkv_S[l],
        sh=jnp.pad(state_rwkv_shift[l], ((0, 0), (0, RW_PACK - RW_COLS)))[:, None, :],
        cb=state_ffn_conv[l],
    ) for l in range(DEPTH)]

    y_p, sp = _trunk(x_prompt, zero_states, lw, fg, sample=False)
    y_s, ss = _trunk(x_sample, samp_states, lw, fg, sample=True)
    return (y_p, y_s, sp["C"], sp["n"], sp["m"], sp["S"], sp["sh"], sp["cb"],
            ss["C"], ss["n"], ss["m"], ss["S"], ss["sh"], ss["cb"], ss["v"])
```

```python
import functools

import jax
import jax.numpy as jnp
from jax import lax
from jax.experimental import pallas as pl
from jax.experimental.pallas import tpu as pltpu

F32 = jnp.float32
BF16 = jnp.bfloat16

D_MODEL = 2048
DEPTH = 2
ML_HEADS = 4
ML_DIM = 128
ML_WIDTH = 512
ML_CHUNK = 64
RW_HEAD = 64
RW_WIDTH = 1024
RW_HEADS = 16
RW_COLS = 3 * RW_WIDTH + 64 + 64 + 128
ML_COLS = 4 * ML_WIDTH + 2 * ML_HEADS
SG_WIDTH = 512
SG_GROUPS = 4
SG_CHUNK = 128
D_FF = 5632
EPS = 1e-6
RW_GN_EPS = RW_HEAD * 1e-5

TN_IN = 512
ML_PACK = 2560
RW_PACK = 3584
SG_PACK = 1024
N_PACK = ML_PACK + RW_PACK + SG_PACK
ML_TILES = ML_PACK // TN_IN
RW_TILES = RW_PACK // TN_IN
SG_TILES = SG_PACK // TN_IN

RW_CHUNK = 64
SAMPLE_T_PAD = 8
TF_FFN = 512
VMEM_LIMIT = 56 * 1024 * 1024


def _cparams(sem):
    return pltpu.CompilerParams(dimension_semantics=sem, vmem_limit_bytes=VMEM_LIMIT)


def _dot(a, b):
    return jnp.dot(a, b, preferred_element_type=F32)


def _rms(x, g):
    return x * lax.rsqrt(jnp.mean(x * x, axis=-1, keepdims=True) + EPS) * g


def _sigmoid(x):
    return 1.0 / (1.0 + jnp.exp(-x))


def _softplus(x):
    return jnp.maximum(x, 0.0) + jnp.log1p(jnp.exp(-jnp.abs(x)))


def _in_proj_kernel(x_ref, g_ref, w_ref, ml_ref, rw_ref, sg_ref, xn_ref):
    j = pl.program_id(1)

    @pl.when(j == 0)
    def _():
        xn_ref[...] = _rms(x_ref[...], g_ref[...]).astype(BF16)

    res = _dot(xn_ref[...], w_ref[...])

    @pl.when(j < ML_TILES)
    def _():
        ml_ref[...] = res

    @pl.when(jnp.logical_and(j >= ML_TILES, j < ML_TILES + RW_TILES))
    def _():
        rw_ref[...] = res

    @pl.when(j >= ML_TILES + RW_TILES)
    def _():
        sg_ref[...] = res


def _in_proj(x, g, w_packed, tm):
    m = x.shape[0]
    n_tiles = N_PACK // TN_IN
    return pl.pallas_call(
        _in_proj_kernel,
        grid=(m // tm, n_tiles),
        in_specs=[
            pl.BlockSpec((tm, D_MODEL), lambda i, j: (i, 0)),
            pl.BlockSpec((1, D_MODEL), lambda i, j: (0, 0)),
            pl.BlockSpec((D_MODEL, TN_IN), lambda i, j: (0, j)),
        ],
        out_specs=[
            pl.BlockSpec((tm, TN_IN), lambda i, j: (i, jnp.minimum(j, ML_TILES - 1))),
            pl.BlockSpec((tm, TN_IN), lambda i, j: (i, jnp.clip(j - ML_TILES, 0, RW_TILES - 1))),
            pl.BlockSpec((tm, TN_IN), lambda i, j: (i, jnp.clip(j - ML_TILES - RW_TILES, 0, SG_TILES - 1))),
        ],
        out_shape=[
            jax.ShapeDtypeStruct((m, ML_PACK), F32),
            jax.ShapeDtypeStruct((m, RW_PACK), F32),
            jax.ShapeDtypeStruct((m, SG_PACK), F32),
        ],
        scratch_shapes=[pltpu.VMEM((tm, D_MODEL), BF16)],
        compiler_params=_cparams(("parallel", "arbitrary")),
        name="in_proj",
    )(x, g, w_packed)


def _mlstm_kernel(q_ref, k_ref, v_ref, o_ref, gc_ref, gr_ref, bic_ref, bir_ref, ng_ref,
                  c0_ref, n0_ref, m0_ref,
                  y_ref, cout_ref, nout_ref, mout_ref,
                  c_scr, n_scr, m_scr, *, chunk, t_valid):
    L = chunk
    c_idx = pl.program_id(1)

    @pl.when(c_idx == 0)
    def _():
        c_scr[...] = c0_ref[0]
        n_scr[...] = n0_ref[0]
        m_scr[...] = m0_ref[0]

    row = lax.broadcasted_iota(jnp.int32, (L, L), 0)
    col = lax.broadcasted_iota(jnp.int32, (L, L), 1)
    causal = row >= col
    gcol = gc_ref[0] + bir_ref[...]
    grow = gr_ref[0, 0] + bic_ref[...]
    rvalid = lax.broadcasted_iota(jnp.int32, (L, 1), 0) < t_valid
    cvalid = lax.broadcasted_iota(jnp.int32, (1, L), 1) < t_valid
    scale = ML_DIM ** -0.5

    for h in range(ML_HEADS):
        sl = slice(h * ML_DIM, (h + 1) * ML_DIM)
        q = q_ref[0, :, sl]
        k = k_ref[0, :, sl] * scale
        v = v_ref[0, :, sl]
        ig_c = gcol[:, h:h + 1]
        f_c = gcol[:, ML_HEADS + h:ML_HEADS + h + 1]
        ig_r = grow[h:h + 1, :]
        f_r = grow[ML_HEADS + h:ML_HEADS + h + 1, :]
        lf_c = -_softplus(-f_c)
        lf_r = -_softplus(-f_r)
        if t_valid < L:
            ig_c = jnp.where(rvalid, ig_c, -jnp.inf)
            ig_r = jnp.where(cvalid, ig_r, -jnp.inf)
            lf_c = jnp.where(rvalid, lf_c, 0.0)
            lf_r = jnp.where(cvalid, lf_r, 0.0)
        b_c = jnp.sum(jnp.where(causal, lf_r, 0.0), axis=-1, keepdims=True)
        b_r = jnp.sum(jnp.where(row <= col, lf_c, 0.0), axis=0, keepdims=True)
        b_last = jnp.sum(lf_r, axis=-1, keepdims=True)
        m_prev = m_scr[h][:, 0:1]
        C = c_scr[h]
        n_row = n_scr[h]

        inter = b_c + m_prev
        dmat = jnp.where(causal, b_c - b_r + ig_r, -jnp.inf)
        m_row = jnp.maximum(inter, jnp.max(dmat, axis=-1, keepdims=True))
        s_inter = jnp.exp(inter - m_row)
        qb = q.astype(BF16)
        kb = k.astype(BF16)
        vb = v.astype(BF16)
        qk = lax.dot_general(qb, kb, (((1,), (1,)), ((), ())), preferred_element_type=F32)
        att = qk * jnp.exp(dmat - m_row)
        num = s_inter * _dot(qb, C.astype(BF16)) + _dot(att.astype(BF16), vb)
        den = s_inter * jnp.sum(q * n_row, axis=-1, keepdims=True) + jnp.sum(att, axis=-1, keepdims=True)
        hh = num / jnp.maximum(jnp.abs(den), jnp.exp(-m_row))

        g_c = b_last - b_c + ig_c
        m_new = jnp.maximum(b_last + m_prev, jnp.max(g_c, axis=0, keepdims=True))
        wk_c = jnp.exp(g_c - m_new)
        dec = jnp.exp(b_last + m_prev - m_new)
        kw = k * wk_c
        c_scr[h] = dec * C + lax.dot_general(kw.astype(BF16), vb, (((0,), (0,)), ((), ())),
                                             preferred_element_type=F32)
        n_scr[h] = dec * n_row + jnp.sum(kw, axis=0, keepdims=True)
        m_scr[h] = jnp.broadcast_to(m_new, (1, ML_DIM))

        hn = hh * lax.rsqrt(jnp.mean(hh * hh, axis=-1, keepdims=True) + EPS)
        y_ref[0, :, sl] = hn * ng_ref[:, sl] * _sigmoid(o_ref[0, :, sl])

    cout_ref[0] = c_scr[...]
    nout_ref[0] = n_scr[...]
    mout_ref[0] = m_scr[...]


def _mlstm(p_ml3, g_rows, b_col, b_row, norm_g, c0, n0, m0, chunk, t_valid):
    bg, tp, _ = p_ml3.shape
    nc = tp // chunk
    L = chunk
    kern = functools.partial(_mlstm_kernel, chunk=chunk, t_valid=t_valid)
    st4 = lambda b, c: (b, 0, 0, 0)
    return pl.pallas_call(
        kern,
        grid=(bg, nc),
        in_specs=[
            pl.BlockSpec((1, L, ML_WIDTH), lambda b, c: (b, c, 0)),
            pl.BlockSpec((1, L, ML_WIDTH), lambda b, c: (b, c, 1)),
            pl.BlockSpec((1, L, ML_WIDTH), lambda b, c: (b, c, 2)),
            pl.BlockSpec((1, L, ML_WIDTH), lambda b, c: (b, c, 3)),
            pl.BlockSpec((1, L, ML_WIDTH), lambda b, c: (b, c, 4)),
            pl.BlockSpec((1, 1, 2 * ML_HEADS, L), lambda b, c: (b, c, 0, 0)),
            pl.BlockSpec((2 * ML_HEADS, 1), lambda b, c: (0, 0)),
            pl.BlockSpec((1, ML_WIDTH), lambda b, c: (0, 0)),
            pl.BlockSpec((1, ML_WIDTH), lambda b, c: (0, 0)),
            pl.BlockSpec((1, ML_HEADS, ML_DIM, ML_DIM), st4),
            pl.BlockSpec((1, ML_HEADS, 1, ML_DIM), st4),
            pl.BlockSpec((1, ML_HEADS, 1, ML_DIM), st4),
        ],
        out_specs=[
            pl.BlockSpec((1, L, ML_WIDTH), lambda b, c: (b, c, 0)),
            pl.BlockSpec((1, ML_HEADS, ML_DIM, ML_DIM), st4),
            pl.BlockSpec((1, ML_HEADS, 1, ML_DIM), st4),
            pl.BlockSpec((1, ML_HEADS, 1, ML_DIM), st4),
        ],
        out_shape=[
            jax.ShapeDtypeStruct((bg, tp, ML_WIDTH), F32),
            jax.ShapeDtypeStruct((bg, ML_HEADS, ML_DIM, ML_DIM), F32),
            jax.ShapeDtypeStruct((bg, ML_HEADS, 1, ML_DIM), F32),
            jax.ShapeDtypeStruct((bg, ML_HEADS, 1, ML_DIM), F32),
        ],
        scratch_shapes=[
            pltpu.VMEM((ML_HEADS, ML_DIM, ML_DIM), F32),
            pltpu.VMEM((ML_HEADS, 1, ML_DIM), F32),
            pltpu.VMEM((ML_HEADS, 1, ML_DIM), F32),
        ],
        compiler_params=_cparams(("parallel", "arbitrary")),
        name="mlstm",
    )(p_ml3, p_ml3, p_ml3, p_ml3, p_ml3, g_rows, b_col, b_row, norm_g, c0, n0, m0)


def _split2(x):
    hi = x.astype(BF16)
    lo = (x - hi.astype(F32)).astype(BF16)
    return hi, lo


def _split3(x):
    hi = x.astype(BF16)
    r1 = x - hi.astype(F32)
    mid = r1.astype(BF16)
    lo = (r1 - mid.astype(F32)).astype(BF16)
    return hi, mid, lo


def _bd(q, m0b, m1b):
    return jnp.concatenate([q * m0b, q * m1b], axis=0)


def _pp(p, qs, m0b, m1b, prec):
    if prec == 1:
        lhs = p.astype(BF16)
        rhs = jnp.concatenate([_bd(q.astype(BF16), m0b, m1b) for q in qs], axis=1)
    else:
        ph, pl_ = _split2(p)
        lhs = jnp.concatenate([ph, ph, pl_], axis=1)
        hs, ls = zip(*[_split2(q) for q in qs])
        bh = jnp.concatenate([_bd(q, m0b, m1b) for q in hs], axis=1)
        bl = jnp.concatenate([_bd(q, m0b, m1b) for q in ls], axis=1)
        rhs = jnp.concatenate([bh, bl, bh], axis=0)
    return jnp.dot(lhs, rhs, preferred_element_type=F32)


def _nt(x, z_h, z_l, prec):
    dn = (((1,), (1,)), ((), ()))
    if prec == 1:
        return lax.dot_general(x.astype(BF16), z_h, dn, preferred_element_type=F32)
    xh, xl = _split2(x)
    lhs = jnp.concatenate([xh, xh, xl], axis=1)
    rhs = jnp.concatenate([z_h, z_l, z_h], axis=1)
    return lax.dot_general(lhs, rhs, dn, preferred_element_type=F32)


def _tn_diag(a, b, m0, prec):
    dn = (((0,), (0,)), ((), ()))
    if prec == 1:
        c = lax.dot_general(a.astype(BF16), b.astype(BF16), dn, preferred_element_type=F32)
    else:
        ah, al = _split2(a)
        bh, bl = _split2(b)
        c = lax.dot_general(jnp.concatenate([ah, ah, al], axis=0), jnp.concatenate([bh, bl, bh], axis=0),
                            dn, preferred_element_type=F32)
    return jnp.where(m0, c[:RW_HEAD], c[RW_HEAD:])


def _seg_sum(x, m0):
    s0 = jnp.sum(jnp.where(m0, x, 0.0), axis=-1, keepdims=True)
    s1 = jnp.sum(jnp.where(m0, 0.0, x), axis=-1, keepdims=True)
    return jnp.where(m0, s0, s1)


def _tri_inverse_pair(n, row, colp, eye, m0b, m1b):
    def prod(ps, qs, prec=1):
        return [_pp(p, [q], m0b, m1b, prec) for p, q in zip(ps, qs)]

    blk8 = (row >> 3) == (colp >> 3)
    d = [jnp.where(blk8, x, 0.0) for x in n]
    d2 = prod(d, d)
    d4 = prod(d2, d2)
    d3 = prod(d, d2)
    p1 = [eye + a + b + c for a, b, c in zip(d, d2, d3)]
    t = [a + b for a, b in zip(p1, prod(p1, d4))]
    for sh in (3, 4, 5):
        same = (row >> (sh + 1)) == (colp >> (sh + 1))
        low_left = jnp.logical_and(((row >> sh) & 1) == 1, ((colp >> sh) & 1) == 0)
        msk = jnp.logical_and(same, low_left)
        e = [jnp.where(msk, x, 0.0) for x in n]
        t = [a + b for a, b in zip(t, prod(prod(t, e), t))]
    resid = [eye - a + b for a, b in zip(t, prod(n, t, 3))]
    return [a + b for a, b in zip(t, prod(t, resid))]


def _rwkv_prep(p, prev, mu_ref, w0_ref, wup_ref, a0_ref, aup_ref, gup_ref, kk_ref, ka_ref, rk_ref):
    px = p + (prev - p) * mu_ref[...]
    W = RW_WIDTH
    r = px[:, 0:W]
    k = px[:, W:2 * W]
    v = px[:, 2 * W:3 * W]
    xw = px[:, 3 * W:3 * W + 64]
    xa = px[:, 3 * W + 64:3 * W + 128]
    xg = px[:, 3 * W + 128:3 * W + 256]
    w_log = -_softplus(-(w0_ref[...] + _dot(jnp.tanh(xw).astype(BF16), wup_ref[...]))) - 0.5
    logw = -jnp.exp(w_log)
    alr = _sigmoid(a0_ref[...] + _dot(xa.astype(BF16), aup_ref[...]))
    gate = _dot(_sigmoid(xg).astype(BF16), gup_ref[...])
    kk_raw = k * kk_ref[...]
    k_mod = k * (1.0 + (alr - 1.0) * ka_ref[...])
    rkp = r * k_mod * rk_ref[...]
    return r, k_mod, v, logw, alr, kk_raw, gate, rkp


def _rwkv_prompt_kernel(p_ref, mu_ref, w0_ref, wup_ref, a0_ref, aup_ref, gup_ref,
                        kk_ref, ka_ref, rk_ref, lng_ref, lnb_ref,
                        y_ref, sout_ref, s_scr, carry_scr, *, prec):
    L = RW_CHUNK
    NP = RW_HEADS // 2
    c_idx = pl.program_id(1)

    @pl.when(c_idx == 0)
    def _():
        s_scr[...] = jnp.zeros_like(s_scr)
        carry_scr[...] = jnp.zeros_like(carry_scr)

    p = p_ref[0]
    rowp = lax.broadcasted_iota(jnp.int32, (L, 1), 0)
    prev = jnp.where(rowp == 0, carry_scr[0:1, :], pltpu.roll(p, 1, 0))
    carry_scr[0:1, :] = p[L - 1:L, :]
    r, k_mod, v, logw, alr, kk_raw, gate, rkp = _rwkv_prep(
        p, prev, mu_ref, w0_ref, wup_ref, a0_ref, aup_ref, gup_ref, kk_ref, ka_ref, rk_ref)

    row = lax.broadcasted_iota(jnp.int32, (L, 128), 0)
    lane = lax.broadcasted_iota(jnp.int32, (L, 128), 1)
    colp = lane & (RW_HEAD - 1)
    m0 = lane < RW_HEAD
    lane1 = lax.broadcasted_iota(jnp.int32, (1, 128), 1)
    m0b = (lane1 < RW_HEAD).astype(BF16)
    m1b = (lane1 >= RW_HEAD).astype(BF16)
    strict = row > colp
    incl = row >= colp
    eye = jnp.where(row == colp, 1.0, 0.0)

    trow = lax.broadcasted_iota(jnp.int32, (L, L), 0)
    tcol = lax.broadcasted_iota(jnp.int32, (L, L), 1)
    tri = jnp.where(trow >= tcol, 1.0, 0.0).astype(BF16)
    lw_h, lw_m, lw_l = _split3(logw)
    c_incl = jnp.dot(jnp.concatenate([tri, tri, tri], axis=1), jnp.concatenate([lw_h, lw_m, lw_l], axis=0),
                     preferred_element_type=F32)
    e_incl = jnp.exp(c_incl)
    e_neg = jnp.exp(-c_incl)
    e_excl = jnp.exp(c_incl - logw)
    e_last = e_incl[L - 1:L, :]

    pairs = range(NP)
    sls = [slice(128 * j, 128 * (j + 1)) for j in pairs]
    kk = [kk_raw[:, sl] for sl in sls]
    kk = [x * lax.rsqrt(jnp.maximum(_seg_sum(x * x, m0), 1e-24)) for x in kk]
    v_p = [v[:, sl] for sl in sls]
    at = [-x * e_excl[:, sl] for x, sl in zip(kk, sls)]
    rt = [r[:, sl] * e_incl[:, sl] for sl in sls]
    bt = [x * alr[:, sl] * e_neg[:, sl] for x, sl in zip(kk, sls)]
    kt = [k_mod[:, sl] * e_neg[:, sl] for sl in sls]
    el = [e_last[:, sl] for sl in sls]

    def gram(a, rr, b, k):
        b_h, b_l = _split2(b)
        k_h, k_l = _split2(k)
        z_h = jnp.concatenate([_bd(b_h, m0b, m1b), _bd(k_h, m0b, m1b)], axis=0)
        z_l = jnp.concatenate([_bd(b_l, m0b, m1b), _bd(k_l, m0b, m1b)], axis=0)
        return _nt(jnp.concatenate([a, rr], axis=0), z_h, z_l, 3)

    g = [gram(*xs) for xs in zip(at, rt, bt, kt)]
    a_ab = [jnp.where(strict, x[:L, :128], 0.0) for x in g]
    a_ak = [jnp.where(strict, x[:L, 128:], 0.0) for x in g]
    q_b = [jnp.where(incl, x[L:, :128], 0.0) for x in g]
    q_k = [jnp.where(incl, x[L:, 128:], 0.0) for x in g]
    tinv = _tri_inverse_pair(a_ab, row, colp, eye, m0b, m1b)
    akqk_v = [_pp(jnp.concatenate([a, q], axis=0), [vv], m0b, m1b, prec)
              for a, q, vv in zip(a_ak, q_k, v_p)]
    wu = [_pp(t, [a, x[:L]], m0b, m1b, prec) for t, a, x in zip(tinv, at, akqk_v)]
    kv = [_tn_diag(vv, k * e, m0, prec) for vv, k, e in zip(v_p, kt, el)]

    def with_state(j, w, rr):
        s_h, s_l = _split2(s_scr[j])
        return _nt(jnp.concatenate([w[:, :128], rr], axis=0), _bd(s_h, m0b, m1b), _bd(s_l, m0b, m1b), prec)

    wr = [with_state(j, w, rr) for j, w, rr in zip(pairs, wu, rt)]
    u = [a[:L] + w[:, 128:] for a, w in zip(wr, wu)]
    qbu = [_pp(q, [x], m0b, m1b, prec) for q, x in zip(q_b, u)]
    ub = [_tn_diag(x, b * e, m0, prec) for x, b, e in zip(u, bt, el)]
    for j in pairs:
        s_scr[j] = s_scr[j] * el[j] + ub[j] + kv[j]
    for j in pairs:
        sl = sls[j]
        y = wr[j][L:] + qbu[j] + akqk_v[j][L:]
        mean = _seg_sum(y, m0) * (1.0 / RW_HEAD)
        yc = y - mean
        var = _seg_sum(yc * yc, m0) * (1.0 / RW_HEAD)
        yn = yc * lax.rsqrt(var + RW_GN_EPS)
        bonus = _seg_sum(rkp[:, sl], m0) * v_p[j]
        y_ref[0, :, sl] = (yn * lng_ref[:, sl] + lnb_ref[:, sl] + bonus) * gate[:, sl]

    @pl.when(c_idx == pl.num_programs(1) - 1)
    def _():
        for j in range(NP):
            S = s_scr[j]
            sout_ref[0, 2 * j] = S[:, :RW_HEAD]
            sout_ref[0, 2 * j + 1] = S[:, RW_HEAD:]


def _rwkv_prompt(p_rw3, prm, prec=1):
    bg, t, _ = p_rw3.shape
    L = RW_CHUNK
    full = lambda shape: pl.BlockSpec(shape, lambda b, c: tuple(0 for _ in shape))
    return pl.pallas_call(
        functools.partial(_rwkv_prompt_kernel, prec=prec),
        grid=(bg, t // L),
        in_specs=[
            pl.BlockSpec((1, L, RW_PACK), lambda b, c: (b, c, 0)),
            full((1, RW_PACK)),
            full((1, RW_WIDTH)), full((64, RW_WIDTH)),
            full((1, RW_WIDTH)), full((64, RW_WIDTH)),
            full((128, RW_WIDTH)),
            full((1, RW_WIDTH)), full((1, RW_WIDTH)), full((1, RW_WIDTH)),
            full((1, RW_WIDTH)), full((1, RW_WIDTH)),
        ],
        out_specs=[
            pl.BlockSpec((1, L, RW_WIDTH), lambda b, c: (b, c, 0)),
            pl.BlockSpec((1, RW_HEADS, RW_HEAD, RW_HEAD), lambda b, c: (b, 0, 0, 0)),
        ],
        out_shape=[
            jax.ShapeDtypeStruct((bg, t, RW_WIDTH), F32),
            jax.ShapeDtypeStruct((bg, RW_HEADS, RW_HEAD, RW_HEAD), F32),
        ],
        scratch_shapes=[
            pltpu.VMEM((RW_HEADS // 2, RW_HEAD, 128), F32),
            pltpu.VMEM((8, RW_PACK), F32),
        ],
        compiler_params=_cparams(("parallel", "arbitrary")),
        name="rwkv7_prompt",
    )(p_rw3, prm["mu"], prm["w0"], prm["w_up"], prm["a0"], prm["a_up"], prm["g_up"],
      prm["k_k"], prm["k_a"], prm["r_k"], prm["ln_g"], prm["ln_b"])


SAMPLE_GB = 16
SAMPLE_UNROLL = 8


def _rwkv_sample_kernel(p_ref, prev0_ref, s0_ref, mu_ref, w0_ref, wup_ref, a0_ref, aup_ref, gup_ref,
                        kk_ref, ka_ref, rk_ref, lng_ref, lnb_ref,
                        y_ref, sout_ref,
                        a_s, w_s, b_s, k_s, r_s, vt_s, yn_s, *, t_seq):
    R = SAMPLE_GB * t_seq
    p = p_ref[...]
    rowp = lax.broadcasted_iota(jnp.int32, (R, 1), 0)
    prev = jnp.where((rowp & (t_seq - 1)) == 0, prev0_ref[...], pltpu.roll(p, 1, 0))
    r, k_mod, v, logw, alr, kk_raw, gate, rkp = _rwkv_prep(
        p, prev, mu_ref, w0_ref, wup_ref, a0_ref, aup_ref, gup_ref, kk_ref, ka_ref, rk_ref)
    decay = jnp.exp(logw)
    bonus = []
    for h in range(RW_HEADS):
        sl = slice(h * RW_HEAD, (h + 1) * RW_HEAD)
        kk = kk_raw[:, sl]
        kk = kk * lax.rsqrt(jnp.maximum(jnp.sum(kk * kk, axis=-1, keepdims=True), 1e-24))
        a_s[h] = -kk
        b_s[h] = kk * alr[:, sl]
        w_s[h] = decay[:, sl]
        k_s[h] = k_mod[:, sl]
        r_s[h] = r[:, sl]
        vt_s[h] = v[:, sl].T
        bonus.append(jnp.sum(rkp[:, sl], axis=-1, keepdims=True) * v[:, sl])

    lane_r = lax.broadcasted_iota(jnp.int32, (1, R), 1)

    def head(h, carry):
        vt = vt_s[h]

        def group(gi, yt):
            bs_ = [gi * SAMPLE_UNROLL + bb for bb in range(SAMPLE_UNROLL)]
            S = [s0_ref[b, h] for b in bs_]
            for t in range(t_seq):
                idx = [b * t_seq + t for b in bs_]
                onehot = [lane_r == i for i in idx]
                sa = [jnp.sum(s * a_s[h, pl.ds(i, 1), :], axis=-1, keepdims=True) for s, i in zip(S, idx)]
                v_col = [jnp.sum(jnp.where(oh, vt, 0.0), axis=-1, keepdims=True) for oh in onehot]
                S = [s * w_s[h, pl.ds(i, 1), :] + x * b_s[h, pl.ds(i, 1), :] + vc * k_s[h, pl.ds(i, 1), :]
                     for s, x, vc, i in zip(S, sa, v_col, idx)]
                y_col = [jnp.sum(s * r_s[h, pl.ds(i, 1), :], axis=-1, keepdims=True) for s, i in zip(S, idx)]
                for oh, yc in zip(onehot, y_col):
                    yt = yt + jnp.where(oh, yc, 0.0)
            for b, s in zip(bs_, S):
                sout_ref[b, h] = s
            return yt

        yt = lax.fori_loop(0, SAMPLE_GB // SAMPLE_UNROLL, group, jnp.zeros((RW_HEAD, R), F32))
        y = yt.T
        mean = jnp.mean(y, axis=-1, keepdims=True)
        yc = y - mean
        var = jnp.mean(yc * yc, axis=-1, keepdims=True)
        yn_s[h] = yc * lax.rsqrt(var + RW_GN_EPS)
        return carry

    lax.fori_loop(0, RW_HEADS, head, 0)
    for h in range(RW_HEADS):
        sl = slice(h * RW_HEAD, (h + 1) * RW_HEAD)
        y_ref[:, sl] = (yn_s[h] * lng_ref[:, sl] + lnb_ref[:, sl] + bonus[h]) * gate[:, sl]


def _rwkv_sample(p_rw, prev0, s0, prm, t_seq):
    m = p_rw.shape[0]
    R = SAMPLE_GB * t_seq
    full = lambda shape: pl.BlockSpec(shape, lambda i: tuple(0 for _ in shape))
    head_rows = pltpu.VMEM((RW_HEADS, R, RW_HEAD), F32)
    return pl.pallas_call(
        functools.partial(_rwkv_sample_kernel, t_seq=t_seq),
        grid=(m // R,),
        in_specs=[
            pl.BlockSpec((R, RW_PACK), lambda i: (i, 0)),
            pl.BlockSpec((R, RW_PACK), lambda i: (i, 0)),
            pl.BlockSpec((SAMPLE_GB, RW_HEADS, RW_HEAD, RW_HEAD), lambda i: (i, 0, 0, 0)),
            full((1, RW_PACK)),
            full((1, RW_WIDTH)), full((64, RW_WIDTH)),
            full((1, RW_WIDTH)), full((64, RW_WIDTH)),
            full((128, RW_WIDTH)),
            full((1, RW_WIDTH)), full((1, RW_WIDTH)), full((1, RW_WIDTH)),
            full((1, RW_WIDTH)), full((1, RW_WIDTH)),
        ],
        out_specs=[
            pl.BlockSpec((R, RW_WIDTH), lambda i: (i, 0)),
            pl.BlockSpec((SAMPLE_GB, RW_HEADS, RW_HEAD, RW_HEAD), lambda i: (i, 0, 0, 0)),
        ],
        out_shape=[
            jax.ShapeDtypeStruct((m, RW_WIDTH), F32),
            jax.ShapeDtypeStruct((m // t_seq, RW_HEADS, RW_HEAD, RW_HEAD), F32),
        ],
        scratch_shapes=[head_rows] * 5 + [
            pltpu.VMEM((RW_HEADS, RW_HEAD, R), F32),
            head_rows,
        ],
        compiler_params=_cparams(("parallel",)),
        name="rwkv7_sample",
    )(p_rw, prev0, s0, prm["mu"], prm["w0"], prm["w_up"], prm["a0"], prm["a_up"], prm["g_up"],
      prm["k_k"], prm["k_a"], prm["r_k"], prm["ln_g"], prm["ln_b"])


def _sgu_kernel(p_ref, lng_ref, lnb_ref, w_ref, bs_ref, y_ref, v_ref):
    x = p_ref[...]
    z = 0.5 * x * (1.0 + lax.erf(x * (2.0 ** -0.5)))
    row = lax.broadcasted_iota(jnp.int32, (SG_CHUNK, SG_CHUNK), 0)
    col = lax.broadcasted_iota(jnp.int32, (SG_CHUNK, SG_CHUNK), 1)
    causal = row >= col
    for g in range(SG_GROUPS):
        sl = slice(g * 128, (g + 1) * 128)
        u = z[:, sl]
        vg = z[:, SG_WIDTH + g * 128:SG_WIDTH + (g + 1) * 128]
        mean = jnp.mean(vg, axis=-1, keepdims=True)
        vc = vg - mean
        var = jnp.mean(vc * vc, axis=-1, keepdims=True)
        vn = vc * lax.rsqrt(var + EPS) * lng_ref[:, sl] + lnb_ref[:, sl]
        wg = jnp.where(causal, w_ref[g], 0.0).astype(BF16)
        mixed = _dot(wg, vn.astype(BF16)) + bs_ref[:, g:g + 1]
        y_ref[:, sl] = u * mixed
        v_ref[:, sl] = vn


def _sgu(p_sg, ln_g, ln_b, w_mix, bs_col):
    m = p_sg.shape[0]
    return pl.pallas_call(
        _sgu_kernel,
        grid=(m // SG_CHUNK,),
        in_specs=[
            pl.BlockSpec((SG_CHUNK, SG_PACK), lambda i: (i, 0)),
            pl.BlockSpec((1, SG_WIDTH), lambda i: (0, 0)),
            pl.BlockSpec((1, SG_WIDTH), lambda i: (0, 0)),
            pl.BlockSpec((SG_GROUPS, SG_CHUNK, SG_CHUNK), lambda i: (0, 0, 0)),
            pl.BlockSpec((SG_CHUNK, SG_GROUPS), lambda i: (0, 0)),
        ],
        out_specs=[
            pl.BlockSpec((SG_CHUNK, SG_WIDTH), lambda i: (i, 0)),
            pl.BlockSpec((SG_CHUNK, SG_WIDTH), lambda i: (i, 0)),
        ],
        out_shape=[
            jax.ShapeDtypeStruct((m, SG_WIDTH), F32),
            jax.ShapeDtypeStruct((m, SG_WIDTH), F32),
        ],
        compiler_params=_cparams(("parallel",)),
        name="sgu",
    )(p_sg, ln_g, ln_b, w_mix, bs_col)


def _out_proj_kernel(x_ref, ym_ref, yr_ref, ys_ref, wm_ref, wr_ref, ws_ref, o_ref):
    acc = _dot(ym_ref[...].astype(BF16), wm_ref[...])
    acc = acc + _dot(yr_ref[...].astype(BF16), wr_ref[...])
    acc = acc + _dot(ys_ref[...].astype(BF16), ws_ref[...])
    o_ref[...] = x_ref[...] + acc


def _out_proj(x, ym, yr, ys, w_m, w_r, w_s, tm):
    m = x.shape[0]
    return pl.pallas_call(
        _out_proj_kernel,
        grid=(m // tm,),
        in_specs=[
            pl.BlockSpec((tm, D_MODEL), lambda i: (i, 0)),
            pl.BlockSpec((tm, ML_WIDTH), lambda i: (i, 0)),
            pl.BlockSpec((tm, RW_WIDTH), lambda i: (i, 0)),
            pl.BlockSpec((tm, SG_WIDTH), lambda i: (i, 0)),
            pl.BlockSpec((ML_WIDTH, D_MODEL), lambda i: (0, 0)),
            pl.BlockSpec((RW_WIDTH, D_MODEL), lambda i: (0, 0)),
            pl.BlockSpec((SG_WIDTH, D_MODEL), lambda i: (0, 0)),
        ],
        out_specs=pl.BlockSpec((tm, D_MODEL), lambda i: (i, 0)),
        out_shape=jax.ShapeDtypeStruct((m, D_MODEL), F32),
        compiler_params=_cparams(("parallel",)),
        name="out_proj",
    )(x, ym, yr, ys, w_m, w_r, w_s)


def _ffn_kernel(*refs, tm, n_f, seq_tiles, sample, final):
    if sample:
        (x_ref, g_ref, wg_ref, wv_ref, cwg_ref, cwv_ref, cbg_ref, cbv_ref, wd_ref, fg_ref,
         e1g_ref, e1v_ref, e2g_ref, e2v_ref,
         o_ref, hg_ref, hv_ref, xn_ref, acc_ref, cg_scr, cv_scr) = refs
    else:
        (x_ref, g_ref, wg_ref, wv_ref, cwg_ref, cwv_ref, cbg_ref, cbv_ref, wd_ref, fg_ref,
         o_ref, hg_ref, hv_ref, xn_ref, acc_ref, cg_scr, cv_scr) = refs
    i = pl.program_id(0)
    f = pl.program_id(1)

    @pl.when(f == 0)
    def _():
        xn_ref[...] = _rms(x_ref[...], g_ref[...]).astype(BF16)
        acc_ref[...] = jnp.zeros_like(acc_ref)

    xn = xn_ref[...]
    hg = _dot(xn, wg_ref[...])
    hv = _dot(xn, wv_ref[...])
    row = lax.broadcasted_iota(jnp.int32, (tm, 1), 0)

    def conv(h, cw_ref, cb_ref, carry_scr, e1_ref, e2_ref):
        r1 = pltpu.roll(h, 1, 0)
        r2 = pltpu.roll(h, 2, 0)
        if sample:
            t = row & 3
            hm1 = jnp.where(t >= 1, r1, e1_ref[...])
            hm2 = jnp.where(t >= 2, r2, e2_ref[...])
        else:
            first = (i % seq_tiles) == 0
            c = carry_scr[f]
            cm2 = jnp.where(first, 0.0, c[0:1, :])
            cm1 = jnp.where(first, 0.0, c[1:2, :])
            hm1 = jnp.where(row == 0, cm1, r1)
            hm2 = jnp.where(row == 0, cm2, jnp.where(row == 1, cm1, r2))
            carry_scr[f, 0:2, :] = h[tm - 2:tm, :]
        return cb_ref[...] + cw_ref[2:3, :] * h + cw_ref[1:2, :] * hm1 + cw_ref[0:1, :] * hm2

    if sample:
        cg = conv(hg, cwg_ref, cbg_ref, cg_scr, e1g_ref, e2g_ref)
        cv = conv(hv, cwv_ref, cbv_ref, cv_scr, e1v_ref, e2v_ref)
        hg_ref[...] = hg
        hv_ref[...] = hv
    else:
        cg = conv(hg, cwg_ref, cbg_ref, cg_scr, None, None)
        cv = conv(hv, cwv_ref, cbv_ref, cv_scr, None, None)
        hg_ref[0] = hg[tm - 2:tm, :]
        hv_ref[0] = hv[tm - 2:tm, :]
    act = (cg * _sigmoid(cg) * cv).astype(BF16)
    acc_ref[...] += _dot(act, wd_ref[...])

    @pl.when(f == n_f - 1)
    def _():
        out = x_ref[...] + acc_ref[...]
        if final:
            out = _rms(out, fg_ref[...])
        o_ref[...] = out


def _ffn(x, g, w_up_bf, conv_w, conv_b, w_down_bf, final_g, *, tm, seq_rows, sample, final, e1=None, e2=None):
    m = x.shape[0]
    tf = TF_FFN
    n_f = D_FF // tf
    n_m = m // tm
    seq_tiles = max(seq_rows // tm, 1)
    bg = m // seq_rows
    kern = functools.partial(_ffn_kernel, tm=tm, n_f=n_f, seq_tiles=seq_tiles, sample=sample, final=final)
    in_specs = [
        pl.BlockSpec((tm, D_MODEL), lambda i, f: (i, 0)),
        pl.BlockSpec((1, D_MODEL), lambda i, f: (0, 0)),
        pl.BlockSpec((D_MODEL, tf), lambda i, f: (0, f)),
        pl.BlockSpec((D_MODEL, tf), lambda i, f: (0, n_f + f)),
        pl.BlockSpec((3, tf), lambda i, f: (0, f)),
        pl.BlockSpec((3, tf), lambda i, f: (0, n_f + f)),
        pl.BlockSpec((1, tf), lambda i, f: (0, f)),
        pl.BlockSpec((1, tf), lambda i, f: (0, n_f + f)),
        pl.BlockSpec((tf, D_MODEL), lambda i, f: (f, 0)),
        pl.BlockSpec((1, D_MODEL), lambda i, f: (0, 0)),
    ]
    args = [x, g, w_up_bf, w_up_bf, conv_w, conv_w, conv_b, conv_b, w_down_bf, final_g]
    if sample:
        in_specs += [
            pl.BlockSpec((tm, tf), lambda i, f: (i, f)),
            pl.BlockSpec((tm, tf), lambda i, f: (i, n_f + f)),
            pl.BlockSpec((tm, tf), lambda i, f: (i, f)),
            pl.BlockSpec((tm, tf), lambda i, f: (i, n_f + f)),
        ]
        args += [e1, e1, e2, e2]
        h_specs = [pl.BlockSpec((tm, tf), lambda i, f: (i, f))] * 2
        h_shapes = [jax.ShapeDtypeStruct((m, D_FF), F32)] * 2
    else:
        h_specs = [pl.BlockSpec((1, 2, tf), lambda i, f: (i, 0, f))] * 2
        h_shapes = [jax.ShapeDtypeStruct((n_m, 2, D_FF), F32)] * 2
    return pl.pallas_call(
        kern,
        grid=(n_m, n_f),
        in_specs=in_specs,
        out_specs=[pl.BlockSpec((tm, D_MODEL), lambda i, f: (i, 0))] + h_specs,
        out_shape=[jax.ShapeDtypeStruct((m, D_MODEL), F32)] + h_shapes,
        scratch_shapes=[
            pltpu.VMEM((tm, D_MODEL), BF16),
            pltpu.VMEM((tm, D_MODEL), F32),
            pltpu.VMEM((n_f, 8, tf), F32),
            pltpu.VMEM((n_f, 8, tf), F32),
        ],
        compiler_params=_cparams(("arbitrary", "arbitrary")),
        name="conv_ffn",
    )(*args)


def _pack_w_in(w):
    ml = jnp.pad(w[:, :ML_COLS], ((0, 0), (0, ML_PACK - ML_COLS)))
    rw = jnp.pad(w[:, ML_COLS:ML_COLS + RW_COLS], ((0, 0), (0, RW_PACK - RW_COLS)))
    sg = w[:, ML_COLS + RW_COLS:]
    return jnp.concatenate([ml, rw, sg], axis=1).astype(BF16)


def _trunk(x3, states, lw, final_g, *, sample):
    bg, t, _ = x3.shape
    m = bg * t
    tp = SAMPLE_T_PAD if sample else t
    ml_chunk = SAMPLE_T_PAD if sample else ML_CHUNK
    x = x3.reshape(m, D_MODEL)
    outs = dict(C=[], n=[], m=[], S=[], sh=[], cb=[], v=[])
    for l in range(DEPTH):
        w = lw[l]
        st = states[l]
        p_ml, p_rw, p_sg = _in_proj(x, w["mix_g"], w["w_in"], min(m, 1024))

        def seq(a):
            a = a.reshape(bg, t, a.shape[-1])
            if sample:
                a = jnp.pad(a, ((0, 0), (0, tp - t), (0, 0)))
            return a

        p_ml3, p_sg3 = seq(p_ml), seq(p_sg)
        nc_ml = tp // ml_chunk
        g_rows = p_ml3[:, :, 4 * ML_WIDTH:4 * ML_WIDTH + 2 * ML_HEADS]
        g_rows = g_rows.reshape(bg, nc_ml, ml_chunk, 2 * ML_HEADS).transpose(0, 1, 3, 2)
        ym, c_new, n_new, m_new = _mlstm(p_ml3, g_rows, w["ml_b_col"], w["ml_b_row"], w["ml_norm_g"],
                                         st["C"], st["n"], st["m"], ml_chunk, t)
        if sample:
            prev0 = jnp.concatenate([st["sh"], jnp.zeros((bg, t - 1, RW_PACK), F32)], axis=1).reshape(m, RW_PACK)
            yr, s_new = _rwkv_sample(p_rw, prev0, st["S"], w, t)
        else:
            yr, s_new = _rwkv_prompt(p_rw.reshape(bg, t, RW_PACK), w)
            yr = yr.reshape(m, RW_WIDTH)
        ys, vrows = _sgu(p_sg3.reshape(bg * tp, SG_PACK), w["sg_ln_g"], w["sg_ln_b"],
                         w["sg_w_s"] if sample else w["sg_w_p"], w["sg_b_s"] if sample else w["sg_b_p"])

        def unseq(a):
            return a.reshape(bg, tp, a.shape[-1])[:, :t].reshape(m, a.shape[-1])

        x = _out_proj(x, unseq(ym), yr, unseq(ys), w["w_out_m"], w["w_out_r"], w["w_out_s"], 512)
        final = l == DEPTH - 1
        if sample:
            buf = st["cb"]
            e1 = jnp.concatenate([buf[:, 1:2], jnp.zeros((bg, t - 1, 2 * D_FF), F32)], axis=1).reshape(m, 2 * D_FF)
            e2 = jnp.concatenate([buf, jnp.zeros((bg, t - 2, 2 * D_FF), F32)], axis=1).reshape(m, 2 * D_FF)
            x, hg, hv = _ffn(x, w["ffn_g"], w["ffn_w_up"], w["conv_w"], w["conv_b"], w["w_down"], final_g,
                             tm=512, seq_rows=t, sample=True, final=final, e1=e1, e2=e2)
            hfull = jnp.concatenate([hg, hv], axis=-1).reshape(bg, t, 2 * D_FF)
            cb = hfull[:, t - 2:]
        else:
            x, hg, hv = _ffn(x, w["ffn_g"], w["ffn_w_up"], w["conv_w"], w["conv_b"], w["w_down"], final_g,
                             tm=512, seq_rows=t, sample=False, final=final)
            tiles = t // 512
            cb = jnp.concatenate([hg, hv], axis=-1)[tiles - 1::tiles]
        outs["C"].append(c_new)
        outs["n"].append(n_new[:, :, 0, :])
        outs["m"].append(m_new[:, :, 0, 0])
        outs["S"].append(s_new)
        outs["sh"].append(p_rw.reshape(bg, t, RW_PACK)[:, t - 1, :RW_COLS])
        outs["cb"].append(cb)
        outs["v"].append(vrows.reshape(bg, tp, SG_WIDTH)[:, :t])
    y = x.reshape(bg, t, D_MODEL)
    return y, {k: jnp.stack(v) for k, v in outs.items()}


def kernel(x_prompt, x_sample, state_mlstm_C, state_mlstm_n, state_mlstm_m, state_rwkv_S, state_rwkv_shift, state_ffn_conv, mix_norm_g, w_in, mlstm_b_i, mlstm_b_f, mlstm_norm_g, rwkv_mu, rwkv_w0, rwkv_w_up, rwkv_a0, rwkv_a_up, rwkv_g_up, rwkv_k_k, rwkv_k_a, rwkv_r_k, rwkv_ln_g, rwkv_ln_b, sgu_ln_g, sgu_ln_b, sgu_w, sgu_b, w_out, ffn_norm_g, ffn_w_up, ffn_conv_w, ffn_conv_b, ffn_w_down, final_norm_g):
    bp = x_prompt.shape[0]
    bs = x_sample.shape[0]
    lw = []
    reps = SG_CHUNK // SAMPLE_T_PAD
    for l in range(DEPTH):
        b_all = jnp.concatenate([mlstm_b_i[l], mlstm_b_f[l]])
        w8 = sgu_w[l][:, :SAMPLE_T_PAD, :SAMPLE_T_PAD]
        lw.append(dict(
            mix_g=mix_norm_g[l][None, :],
            w_in=_pack_w_in(w_in[l]),
            ml_b_col=b_all[:, None],
            ml_b_row=jnp.pad(b_all, (0, ML_WIDTH - 2 * ML_HEADS))[None, :],
            ml_norm_g=mlstm_norm_g[l][None, :],
            mu=jnp.pad(rwkv_mu[l], (0, RW_PACK - RW_COLS))[None, :],
            w0=rwkv_w0[l][None, :], w_up=rwkv_w_up[l].astype(BF16),
            a0=rwkv_a0[l][None, :], a_up=rwkv_a_up[l].astype(BF16),
            g_up=rwkv_g_up[l].astype(BF16),
            k_k=rwkv_k_k[l][None, :], k_a=rwkv_k_a[l][None, :],
            r_k=rwkv_r_k[l].reshape(1, RW_WIDTH),
            ln_g=rwkv_ln_g[l][None, :], ln_b=rwkv_ln_b[l][None, :],
            sg_ln_g=sgu_ln_g[l][None, :], sg_ln_b=sgu_ln_b[l][None, :],
            sg_w_p=sgu_w[l], sg_b_p=sgu_b[l].T,
            sg_w_s=jax.vmap(lambda a: jnp.kron(jnp.eye(reps, dtype=F32), a))(w8),
            sg_b_s=jnp.tile(sgu_b[l][:, :SAMPLE_T_PAD].T, (reps, 1)),
            w_out_m=w_out[l][:ML_WIDTH].astype(BF16),
            w_out_r=w_out[l][ML_WIDTH:ML_WIDTH + RW_WIDTH].astype(BF16),
            w_out_s=w_out[l][ML_WIDTH + RW_WIDTH:].astype(BF16),
            ffn_g=ffn_norm_g[l][None, :],
            ffn_w_up=ffn_w_up[l].astype(BF16),
            conv_w=ffn_conv_w[l], conv_b=ffn_conv_b[l][None, :],
            w_down=ffn_w_down[l].astype(BF16),
        ))
    fg = final_norm_g[None, :]

    zero_states = [dict(
        C=jnp.zeros((bp, ML_HEADS, ML_DIM, ML_DIM), F32),
        n=jnp.zeros((bp, ML_HEADS, 1, ML_DIM), F32),
        m=jnp.zeros((bp, ML_HEADS, 1, ML_DIM), F32),
        S=jnp.zeros((bp, RW_HEADS, RW_HEAD, RW_HEAD), F32),
        sh=jnp.zeros((bp, 1, RW_PACK), F32),
        cb=None,
    ) for _ in range(DEPTH)]
    samp_states = [dict(
        C=state_mlstm_C[l],
        n=state_mlstm_n[l][:, :, None, :],
        m=jnp.broadcast_to(state_mlstm_m[l][:, :, None, None], (bs, ML_HEADS, 1, ML_DIM)),
        S=state_rwkv_S[l],
        sh=jnp.pad(state_rwkv_shift[l], ((0, 0), (0, RW_PACK - RW_COLS)))[:, None, :],
        cb=state_ffn_conv[l],
    ) for l in range(DEPTH)]

    y_p, sp = _trunk(x_prompt, zero_states, lw, fg, sample=False)
    y_s, ss = _trunk(x_sample, samp_states, lw, fg, sample=True)
    return (y_p, y_s, sp["C"], sp["n"], sp["m"], sp["S"], sp["sh"], sp["cb"],
            ss["C"], ss["n"], ss["m"], ss["S"], ss["sh"], ss["cb"], ss["v"])
```

```python
import functools

import jax
import jax.numpy as jnp
from jax import lax
from jax.experimental import pallas as pl
from jax.experimental.pallas import tpu as pltpu

F32 = jnp.float32
BF16 = jnp.bfloat16

D_MODEL = 2048
DEPTH = 2
ML_HEADS = 4
ML_DIM = 128
ML_WIDTH = 512
ML_CHUNK = 64
RW_HEAD = 64
RW_WIDTH = 1024
RW_HEADS = 16
RW_COLS = 3 * RW_WIDTH + 64 + 64 + 128
ML_COLS = 4 * ML_WIDTH + 2 * ML_HEADS
SG_WIDTH = 512
SG_GROUPS = 4
SG_CHUNK = 128
D_FF = 5632
EPS = 1e-6
RW_GN_EPS = RW_HEAD * 1e-5

TN_IN = 512
ML_PACK = 2560
RW_PACK = 3584
SG_PACK = 1024
N_PACK = ML_PACK + RW_PACK + SG_PACK
ML_TILES = ML_PACK // TN_IN
RW_TILES = RW_PACK // TN_IN
SG_TILES = SG_PACK // TN_IN

RW_CHUNK = 64
SAMPLE_T_PAD = 8
TF_FFN = 512
VMEM_LIMIT = 56 * 1024 * 1024


def _cparams(sem):
    return pltpu.CompilerParams(dimension_semantics=sem, vmem_limit_bytes=VMEM_LIMIT)


def _dot(a, b):
    return jnp.dot(a, b, preferred_element_type=F32)


def _rms(x, g):
    return x * lax.rsqrt(jnp.mean(x * x, axis=-1, keepdims=True) + EPS) * g


def _sigmoid(x):
    return 1.0 / (1.0 + jnp.exp(-x))


def _softplus(x):
    return jnp.maximum(x, 0.0) + jnp.log1p(jnp.exp(-jnp.abs(x)))


def _in_proj_kernel(x_ref, g_ref, w_ref, ml_ref, rw_ref, sg_ref, xn_ref):
    j = pl.program_id(1)

    @pl.when(j == 0)
    def _():
        xn_ref[...] = _rms(x_ref[...], g_ref[...]).astype(BF16)

    res = _dot(xn_ref[...], w_ref[...])

    @pl.when(j < ML_TILES)
    def _():
        ml_ref[...] = res

    @pl.when(jnp.logical_and(j >= ML_TILES, j < ML_TILES + RW_TILES))
    def _():
        rw_ref[...] = res

    @pl.when(j >= ML_TILES + RW_TILES)
    def _():
        sg_ref[...] = res


def _in_proj(x, g, w_packed, tm):
    m = x.shape[0]
    n_tiles = N_PACK // TN_IN
    return pl.pallas_call(
        _in_proj_kernel,
        grid=(m // tm, n_tiles),
        in_specs=[
            pl.BlockSpec((tm, D_MODEL), lambda i, j: (i, 0)),
            pl.BlockSpec((1, D_MODEL), lambda i, j: (0, 0)),
            pl.BlockSpec((D_MODEL, TN_IN), lambda i, j: (0, j)),
        ],
        out_specs=[
            pl.BlockSpec((tm, TN_IN), lambda i, j: (i, jnp.minimum(j, ML_TILES - 1))),
            pl.BlockSpec((tm, TN_IN), lambda i, j: (i, jnp.clip(j - ML_TILES, 0, RW_TILES - 1))),
            pl.BlockSpec((tm, TN_IN), lambda i, j: (i, jnp.clip(j - ML_TILES - RW_TILES, 0, SG_TILES - 1))),
        ],
        out_shape=[
            jax.ShapeDtypeStruct((m, ML_PACK), F32),
            jax.ShapeDtypeStruct((m, RW_PACK), F32),
            jax.ShapeDtypeStruct((m, SG_PACK), F32),
        ],
        scratch_shapes=[pltpu.VMEM((tm, D_MODEL), BF16)],
        compiler_params=_cparams(("parallel", "arbitrary")),
        name="in_proj",
    )(x, g, w_packed)


def _mlstm_kernel(q_ref, k_ref, v_ref, o_ref, gc_ref, gr_ref, bic_ref, bir_ref, ng_ref,
                  c0_ref, n0_ref, m0_ref,
                  y_ref, cout_ref, nout_ref, mout_ref,
                  c_scr, n_scr, m_scr, *, chunk, t_valid):
    L = chunk
    c_idx = pl.program_id(1)

    @pl.when(c_idx == 0)
    def _():
        c_scr[...] = c0_ref[0]
        n_scr[...] = n0_ref[0]
        m_scr[...] = m0_ref[0]

    row = lax.broadcasted_iota(jnp.int32, (L, L), 0)
    col = lax.broadcasted_iota(jnp.int32, (L, L), 1)
    causal = row >= col
    gcol = gc_ref[0] + bir_ref[...]
    grow = gr_ref[0, 0] + bic_ref[...]
    rvalid = lax.broadcasted_iota(jnp.int32, (L, 1), 0) < t_valid
    cvalid = lax.broadcasted_iota(jnp.int32, (1, L), 1) < t_valid
    scale = ML_DIM ** -0.5

    heads = range(ML_HEADS)
    sls = [slice(h * ML_DIM, (h + 1) * ML_DIM) for h in heads]
    q = [q_ref[0, :, sl] for sl in sls]
    k = [k_ref[0, :, sl] * scale for sl in sls]
    qb = [x.astype(BF16) for x in q]
    kb = [x.astype(BF16) for x in k]
    vb = [v_ref[0, :, sl].astype(BF16) for sl in sls]
    qk = [lax.dot_general(a, b, (((1,), (1,)), ((), ())), preferred_element_type=F32) for a, b in zip(qb, kb)]
    C = [c_scr[h] for h in heads]
    qc = [_dot(a, c.astype(BF16)) for a, c in zip(qb, C)]

    att, m_row, s_inter, kw, dec, m_new = [], [], [], [], [], []
    for h in heads:
        ig_c = gcol[:, h:h + 1]
        f_c = gcol[:, ML_HEADS + h:ML_HEADS + h + 1]
        ig_r = grow[h:h + 1, :]
        f_r = grow[ML_HEADS + h:ML_HEADS + h + 1, :]
        lf_c = -_softplus(-f_c)
        lf_r = -_softplus(-f_r)
        if t_valid < L:
            ig_c = jnp.where(rvalid, ig_c, -jnp.inf)
            ig_r = jnp.where(cvalid, ig_r, -jnp.inf)
            lf_c = jnp.where(rvalid, lf_c, 0.0)
            lf_r = jnp.where(cvalid, lf_r, 0.0)
        b_c = jnp.sum(jnp.where(causal, lf_r, 0.0), axis=-1, keepdims=True)
        b_r = jnp.sum(jnp.where(row <= col, lf_c, 0.0), axis=0, keepdims=True)
        b_last = jnp.sum(lf_r, axis=-1, keepdims=True)
        m_prev = m_scr[h][:, 0:1]
        inter = b_c + m_prev
        dmat = jnp.where(causal, b_c - b_r + ig_r, -jnp.inf)
        mr = jnp.maximum(inter, jnp.max(dmat, axis=-1, keepdims=True))
        m_row.append(mr)
        s_inter.append(jnp.exp(inter - mr))
        att.append(qk[h] * jnp.exp(dmat - mr))
        g_c = b_last - b_c + ig_c
        mn = jnp.maximum(b_last + m_prev, jnp.max(g_c, axis=0, keepdims=True))
        m_new.append(mn)
        kw.append(k[h] * jnp.exp(g_c - mn))
        dec.append(jnp.exp(b_last + m_prev - mn))

    av = [_dot(a.astype(BF16), b) for a, b in zip(att, vb)]
    kv = [lax.dot_general(a.astype(BF16), b, (((0,), (0,)), ((), ())), preferred_element_type=F32)
          for a, b in zip(kw, vb)]
    for h in heads:
        n_row = n_scr[h]
        num = s_inter[h] * qc[h] + av[h]
        den = (s_inter[h] * jnp.sum(q[h] * n_row, axis=-1, keepdims=True)
               + jnp.sum(att[h], axis=-1, keepdims=True))
        hh = num / jnp.maximum(jnp.abs(den), jnp.exp(-m_row[h]))
        c_scr[h] = dec[h] * C[h] + kv[h]
        n_scr[h] = dec[h] * n_row + jnp.sum(kw[h], axis=0, keepdims=True)
        m_scr[h] = jnp.broadcast_to(m_new[h], (1, ML_DIM))
        hn = hh * lax.rsqrt(jnp.mean(hh * hh, axis=-1, keepdims=True) + EPS)
        y_ref[0, :, sls[h]] = hn * ng_ref[:, sls[h]] * _sigmoid(o_ref[0, :, sls[h]])

    cout_ref[0] = c_scr[...]
    nout_ref[0] = n_scr[...]
    mout_ref[0] = m_scr[...]


def _mlstm(p_ml3, g_rows, b_col, b_row, norm_g, c0, n0, m0, chunk, t_valid):
    bg, tp, _ = p_ml3.shape
    nc = tp // chunk
    L = chunk
    kern = functools.partial(_mlstm_kernel, chunk=chunk, t_valid=t_valid)
    st4 = lambda b, c: (b, 0, 0, 0)
    return pl.pallas_call(
        kern,
        grid=(bg, nc),
        in_specs=[
            pl.BlockSpec((1, L, ML_WIDTH), lambda b, c: (b, c, 0)),
            pl.BlockSpec((1, L, ML_WIDTH), lambda b, c: (b, c, 1)),
            pl.BlockSpec((1, L, ML_WIDTH), lambda b, c: (b, c, 2)),
            pl.BlockSpec((1, L, ML_WIDTH), lambda b, c: (b, c, 3)),
            pl.BlockSpec((1, L, ML_WIDTH), lambda b, c: (b, c, 4)),
            pl.BlockSpec((1, 1, 2 * ML_HEADS, L), lambda b, c: (b, c, 0, 0)),
            pl.BlockSpec((2 * ML_HEADS, 1), lambda b, c: (0, 0)),
            pl.BlockSpec((1, ML_WIDTH), lambda b, c: (0, 0)),
            pl.BlockSpec((1, ML_WIDTH), lambda b, c: (0, 0)),
            pl.BlockSpec((1, ML_HEADS, ML_DIM, ML_DIM), st4),
            pl.BlockSpec((1, ML_HEADS, 1, ML_DIM), st4),
            pl.BlockSpec((1, ML_HEADS, 1, ML_DIM), st4),
        ],
        out_specs=[
            pl.BlockSpec((1, L, ML_WIDTH), lambda b, c: (b, c, 0)),
            pl.BlockSpec((1, ML_HEADS, ML_DIM, ML_DIM), st4),
            pl.BlockSpec((1, ML_HEADS, 1, ML_DIM), st4),
            pl.BlockSpec((1, ML_HEADS, 1, ML_DIM), st4),
        ],
        out_shape=[
            jax.ShapeDtypeStruct((bg, tp, ML_WIDTH), F32),
            jax.ShapeDtypeStruct((bg, ML_HEADS, ML_DIM, ML_DIM), F32),
            jax.ShapeDtypeStruct((bg, ML_HEADS, 1, ML_DIM), F32),
            jax.ShapeDtypeStruct((bg, ML_HEADS, 1, ML_DIM), F32),
        ],
        scratch_shapes=[
            pltpu.VMEM((ML_HEADS, ML_DIM, ML_DIM), F32),
            pltpu.VMEM((ML_HEADS, 1, ML_DIM), F32),
            pltpu.VMEM((ML_HEADS, 1, ML_DIM), F32),
        ],
        compiler_params=_cparams(("parallel", "arbitrary")),
        name="mlstm",
    )(p_ml3, p_ml3, p_ml3, p_ml3, p_ml3, g_rows, b_col, b_row, norm_g, c0, n0, m0)


def _split2(x):
    hi = x.astype(BF16)
    lo = (x - hi.astype(F32)).astype(BF16)
    return hi, lo


def _split3(x):
    hi = x.astype(BF16)
    r1 = x - hi.astype(F32)
    mid = r1.astype(BF16)
    lo = (r1 - mid.astype(F32)).astype(BF16)
    return hi, mid, lo


def _bd(q, m0b, m1b):
    return jnp.concatenate([q * m0b, q * m1b], axis=0)


def _pp(p, qs, m0b, m1b, prec):
    if prec == 1:
        lhs = p.astype(BF16)
        rhs = jnp.concatenate([_bd(q.astype(BF16), m0b, m1b) for q in qs], axis=1)
    else:
        ph, pl_ = _split2(p)
        lhs = jnp.concatenate([ph, ph, pl_], axis=1)
        hs, ls = zip(*[_split2(q) for q in qs])
        bh = jnp.concatenate([_bd(q, m0b, m1b) for q in hs], axis=1)
        bl = jnp.concatenate([_bd(q, m0b, m1b) for q in ls], axis=1)
        rhs = jnp.concatenate([bh, bl, bh], axis=0)
    return jnp.dot(lhs, rhs, preferred_element_type=F32)


def _nt(x, z_h, z_l, prec):
    dn = (((1,), (1,)), ((), ()))
    if prec == 1:
        return lax.dot_general(x.astype(BF16), z_h, dn, preferred_element_type=F32)
    xh, xl = _split2(x)
    lhs = jnp.concatenate([xh, xh, xl], axis=1)
    rhs = jnp.concatenate([z_h, z_l, z_h], axis=1)
    return lax.dot_general(lhs, rhs, dn, preferred_element_type=F32)


def _tn_diag(a, b, m0, prec):
    dn = (((0,), (0,)), ((), ()))
    if prec == 1:
        c = lax.dot_general(a.astype(BF16), b.astype(BF16), dn, preferred_element_type=F32)
    else:
        ah, al = _split2(a)
        bh, bl = _split2(b)
        c = lax.dot_general(jnp.concatenate([ah, ah, al], axis=0), jnp.concatenate([bh, bl, bh], axis=0),
                            dn, preferred_element_type=F32)
    return jnp.where(m0, c[:RW_HEAD], c[RW_HEAD:])


def _seg_sum(x, m0):
    s0 = jnp.sum(jnp.where(m0, x, 0.0), axis=-1, keepdims=True)
    s1 = jnp.sum(jnp.where(m0, 0.0, x), axis=-1, keepdims=True)
    return jnp.where(m0, s0, s1)


def _tri_inverse_pair(n, row, colp, eye, m0b, m1b):
    def prod(ps, qs, prec=1):
        return [_pp(p, [q], m0b, m1b, prec) for p, q in zip(ps, qs)]

    blk8 = (row >> 3) == (colp >> 3)
    d = [jnp.where(blk8, x, 0.0) for x in n]
    d2 = prod(d, d)
    d4 = prod(d2, d2)
    d3 = prod(d, d2)
    p1 = [eye + a + b + c for a, b, c in zip(d, d2, d3)]
    t = [a + b for a, b in zip(p1, prod(p1, d4))]
    for sh in (3, 4, 5):
        same = (row >> (sh + 1)) == (colp >> (sh + 1))
        low_left = jnp.logical_and(((row >> sh) & 1) == 1, ((colp >> sh) & 1) == 0)
        msk = jnp.logical_and(same, low_left)
        e = [jnp.where(msk, x, 0.0) for x in n]
        t = [a + b for a, b in zip(t, prod(prod(t, e), t))]
    resid = [eye - a + b for a, b in zip(t, prod(n, t, 3))]
    return [a + b for a, b in zip(t, prod(t, resid))]


def _rwkv_prep(p, prev, mu_ref, w0_ref, wup_ref, a0_ref, aup_ref, gup_ref, kk_ref, ka_ref, rk_ref):
    px = p + (prev - p) * mu_ref[...]
    W = RW_WIDTH
    r = px[:, 0:W]
    k = px[:, W:2 * W]
    v = px[:, 2 * W:3 * W]
    xw = px[:, 3 * W:3 * W + 64]
    xa = px[:, 3 * W + 64:3 * W + 128]
    xg = px[:, 3 * W + 128:3 * W + 256]
    w_log = -_softplus(-(w0_ref[...] + _dot(jnp.tanh(xw).astype(BF16), wup_ref[...]))) - 0.5
    logw = -jnp.exp(w_log)
    alr = _sigmoid(a0_ref[...] + _dot(xa.astype(BF16), aup_ref[...]))
    gate = _dot(_sigmoid(xg).astype(BF16), gup_ref[...])
    kk_raw = k * kk_ref[...]
    k_mod = k * (1.0 + (alr - 1.0) * ka_ref[...])
    rkp = r * k_mod * rk_ref[...]
    return r, k_mod, v, logw, alr, kk_raw, gate, rkp


def _rwkv_prompt_kernel(p_ref, mu_ref, w0_ref, wup_ref, a0_ref, aup_ref, gup_ref,
                        kk_ref, ka_ref, rk_ref, lng_ref, lnb_ref,
                        y_ref, sout_ref, s_scr, carry_scr, *, prec):
    L = RW_CHUNK
    NP = RW_HEADS // 2
    c_idx = pl.program_id(1)

    @pl.when(c_idx == 0)
    def _():
        s_scr[...] = jnp.zeros_like(s_scr)
        carry_scr[...] = jnp.zeros_like(carry_scr)

    p = p_ref[0]
    rowp = lax.broadcasted_iota(jnp.int32, (L, 1), 0)
    prev = jnp.where(rowp == 0, carry_scr[0:1, :], pltpu.roll(p, 1, 0))
    carry_scr[0:1, :] = p[L - 1:L, :]
    r, k_mod, v, logw, alr, kk_raw, gate, rkp = _rwkv_prep(
        p, prev, mu_ref, w0_ref, wup_ref, a0_ref, aup_ref, gup_ref, kk_ref, ka_ref, rk_ref)

    row = lax.broadcasted_iota(jnp.int32, (L, 128), 0)
    lane = lax.broadcasted_iota(jnp.int32, (L, 128), 1)
    colp = lane & (RW_HEAD - 1)
    m0 = lane < RW_HEAD
    lane1 = lax.broadcasted_iota(jnp.int32, (1, 128), 1)
    m0b = (lane1 < RW_HEAD).astype(BF16)
    m1b = (lane1 >= RW_HEAD).astype(BF16)
    strict = row > colp
    incl = row >= colp
    eye = jnp.where(row == colp, 1.0, 0.0)

    trow = lax.broadcasted_iota(jnp.int32, (L, L), 0)
    tcol = lax.broadcasted_iota(jnp.int32, (L, L), 1)
    tri = jnp.where(trow >= tcol, 1.0, 0.0).astype(BF16)
    lw_h, lw_m, lw_l = _split3(logw)
    c_incl = jnp.dot(jnp.concatenate([tri, tri, tri], axis=1), jnp.concatenate([lw_h, lw_m, lw_l], axis=0),
                     preferred_element_type=F32)
    e_incl = jnp.exp(c_incl)
    e_neg = jnp.exp(-c_incl)
    e_excl = jnp.exp(c_incl - logw)
    e_last = e_incl[L - 1:L, :]

    pairs = range(NP)
    sls = [slice(128 * j, 128 * (j + 1)) for j in pairs]
    kk = [kk_raw[:, sl] for sl in sls]
    kk = [x * lax.rsqrt(jnp.maximum(_seg_sum(x * x, m0), 1e-24)) for x in kk]
    v_p = [v[:, sl] for sl in sls]
    at = [-x * e_excl[:, sl] for x, sl in zip(kk, sls)]
    rt = [r[:, sl] * e_incl[:, sl] for sl in sls]
    bt = [x * alr[:, sl] * e_neg[:, sl] for x, sl in zip(kk, sls)]
    kt = [k_mod[:, sl] * e_neg[:, sl] for sl in sls]
    el = [e_last[:, sl] for sl in sls]

    def gram(a, rr, b, k):
        b_h, b_l = _split2(b)
        k_h, k_l = _split2(k)
        z_h = jnp.concatenate([_bd(b_h, m0b, m1b), _bd(k_h, m0b, m1b)], axis=0)
        z_l = jnp.concatenate([_bd(b_l, m0b, m1b), _bd(k_l, m0b, m1b)], axis=0)
        return _nt(jnp.concatenate([a, rr], axis=0), z_h, z_l, 3)

    g = [gram(*xs) for xs in zip(at, rt, bt, kt)]
    a_ab = [jnp.where(strict, x[:L, :128], 0.0) for x in g]
    a_ak = [jnp.where(strict, x[:L, 128:], 0.0) for x in g]
    q_b = [jnp.where(incl, x[L:, :128], 0.0) for x in g]
    q_k = [jnp.where(incl, x[L:, 128:], 0.0) for x in g]
    tinv = _tri_inverse_pair(a_ab, row, colp, eye, m0b, m1b)
    akqk_v = [_pp(jnp.concatenate([a, q], axis=0), [vv], m0b, m1b, prec)
              for a, q, vv in zip(a_ak, q_k, v_p)]
    wu = [_pp(t, [a, x[:L]], m0b, m1b, prec) for t, a, x in zip(tinv, at, akqk_v)]
    kv = [_tn_diag(vv, k * e, m0, prec) for vv, k, e in zip(v_p, kt, el)]

    def with_state(j, w, rr):
        s_h, s_l = _split2(s_scr[j])
        return _nt(jnp.concatenate([w[:, :128], rr], axis=0), _bd(s_h, m0b, m1b), _bd(s_l, m0b, m1b), prec)

    wr = [with_state(j, w, rr) for j, w, rr in zip(pairs, wu, rt)]
    u = [a[:L] + w[:, 128:] for a, w in zip(wr, wu)]
    qbu = [_pp(q, [x], m0b, m1b, prec) for q, x in zip(q_b, u)]
    ub = [_tn_diag(x, b * e, m0, prec) for x, b, e in zip(u, bt, el)]
    for j in pairs:
        s_scr[j] = s_scr[j] * el[j] + ub[j] + kv[j]
    for j in pairs:
        sl = sls[j]
        y = wr[j][L:] + qbu[j] + akqk_v[j][L:]
        mean = _seg_sum(y, m0) * (1.0 / RW_HEAD)
        yc = y - mean
        var = _seg_sum(yc * yc, m0) * (1.0 / RW_HEAD)
        yn = yc * lax.rsqrt(var + RW_GN_EPS)
        bonus = _seg_sum(rkp[:, sl], m0) * v_p[j]
        y_ref[0, :, sl] = (yn * lng_ref[:, sl] + lnb_ref[:, sl] + bonus) * gate[:, sl]

    @pl.when(c_idx == pl.num_programs(1) - 1)
    def _():
        for j in range(NP):
            S = s_scr[j]
            sout_ref[0, 2 * j] = S[:, :RW_HEAD]
            sout_ref[0, 2 * j + 1] = S[:, RW_HEAD:]


def _rwkv_prompt(p_rw3, prm, prec=1):
    bg, t, _ = p_rw3.shape
    L = RW_CHUNK
    full = lambda shape: pl.BlockSpec(shape, lambda b, c: tuple(0 for _ in shape))
    return pl.pallas_call(
        functools.partial(_rwkv_prompt_kernel, prec=prec),
        grid=(bg, t // L),
        in_specs=[
            pl.BlockSpec((1, L, RW_PACK), lambda b, c: (b, c, 0)),
            full((1, RW_PACK)),
            full((1, RW_WIDTH)), full((64, RW_WIDTH)),
            full((1, RW_WIDTH)), full((64, RW_WIDTH)),
            full((128, RW_WIDTH)),
            full((1, RW_WIDTH)), full((1, RW_WIDTH)), full((1, RW_WIDTH)),
            full((1, RW_WIDTH)), full((1, RW_WIDTH)),
        ],
        out_specs=[
            pl.BlockSpec((1, L, RW_WIDTH), lambda b, c: (b, c, 0)),
            pl.BlockSpec((1, RW_HEADS, RW_HEAD, RW_HEAD), lambda b, c: (b, 0, 0, 0)),
        ],
        out_shape=[
            jax.ShapeDtypeStruct((bg, t, RW_WIDTH), F32),
            jax.ShapeDtypeStruct((bg, RW_HEADS, RW_HEAD, RW_HEAD), F32),
        ],
        scratch_shapes=[
            pltpu.VMEM((RW_HEADS // 2, RW_HEAD, 128), F32),
            pltpu.VMEM((8, RW_PACK), F32),
        ],
        compiler_params=_cparams(("parallel", "arbitrary")),
        name="rwkv7_prompt",
    )(p_rw3, prm["mu"], prm["w0"], prm["w_up"], prm["a0"], prm["a_up"], prm["g_up"],
      prm["k_k"], prm["k_a"], prm["r_k"], prm["ln_g"], prm["ln_b"])


SAMPLE_GB = 16
SAMPLE_UNROLL = 8


def _rwkv_sample_kernel(p_ref, prev0_ref, s0_ref, mu_ref, w0_ref, wup_ref, a0_ref, aup_ref, gup_ref,
                        kk_ref, ka_ref, rk_ref, lng_ref, lnb_ref,
                        y_ref, sout_ref,
                        a_s, w_s, b_s, k_s, r_s, vt_s, yn_s, *, t_seq):
    R = SAMPLE_GB * t_seq
    p = p_ref[...]
    rowp = lax.broadcasted_iota(jnp.int32, (R, 1), 0)
    prev = jnp.where((rowp & (t_seq - 1)) == 0, prev0_ref[...], pltpu.roll(p, 1, 0))
    r, k_mod, v, logw, alr, kk_raw, gate, rkp = _rwkv_prep(
        p, prev, mu_ref, w0_ref, wup_ref, a0_ref, aup_ref, gup_ref, kk_ref, ka_ref, rk_ref)
    decay = jnp.exp(logw)
    bonus = []
    for h in range(RW_HEADS):
        sl = slice(h * RW_HEAD, (h + 1) * RW_HEAD)
        kk = kk_raw[:, sl]
        kk = kk * lax.rsqrt(jnp.maximum(jnp.sum(kk * kk, axis=-1, keepdims=True), 1e-24))
        a_s[h] = -kk
        b_s[h] = kk * alr[:, sl]
        w_s[h] = decay[:, sl]
        k_s[h] = k_mod[:, sl]
        r_s[h] = r[:, sl]
        vt_s[h] = v[:, sl].T
        bonus.append(jnp.sum(rkp[:, sl], axis=-1, keepdims=True) * v[:, sl])

    lane_r = lax.broadcasted_iota(jnp.int32, (1, R), 1)

    def head(h, carry):
        vt = vt_s[h]

        def group(gi, yt):
            bs_ = [gi * SAMPLE_UNROLL + bb for bb in range(SAMPLE_UNROLL)]
            S = [s0_ref[b, h] for b in bs_]
            for t in range(t_seq):
                idx = [b * t_seq + t for b in bs_]
                onehot = [lane_r == i for i in idx]
                sa = [jnp.sum(s * a_s[h, pl.ds(i, 1), :], axis=-1, keepdims=True) for s, i in zip(S, idx)]
                v_col = [jnp.sum(jnp.where(oh, vt, 0.0), axis=-1, keepdims=True) for oh in onehot]
                S = [s * w_s[h, pl.ds(i, 1), :] + x * b_s[h, pl.ds(i, 1), :] + vc * k_s[h, pl.ds(i, 1), :]
                     for s, x, vc, i in zip(S, sa, v_col, idx)]
                y_col = [jnp.sum(s * r_s[h, pl.ds(i, 1), :], axis=-1, keepdims=True) for s, i in zip(S, idx)]
                for oh, yc in zip(onehot, y_col):
                    yt = yt + jnp.where(oh, yc, 0.0)
            for b, s in zip(bs_, S):
                sout_ref[b, h] = s
            return yt

        yt = lax.fori_loop(0, SAMPLE_GB // SAMPLE_UNROLL, group, jnp.zeros((RW_HEAD, R), F32))
        y = yt.T
        mean = jnp.mean(y, axis=-1, keepdims=True)
        yc = y - mean
        var = jnp.mean(yc * yc, axis=-1, keepdims=True)
        yn_s[h] = yc * lax.rsqrt(var + RW_GN_EPS)
        return carry

    lax.fori_loop(0, RW_HEADS, head, 0)
    for h in range(RW_HEADS):
        sl = slice(h * RW_HEAD, (h + 1) * RW_HEAD)
        y_ref[:, sl] = (yn_s[h] * lng_ref[:, sl] + lnb_ref[:, sl] + bonus[h]) * gate[:, sl]


def _rwkv_sample(p_rw, prev0, s0, prm, t_seq):
    m = p_rw.shape[0]
    R = SAMPLE_GB * t_seq
    full = lambda shape: pl.BlockSpec(shape, lambda i: tuple(0 for _ in shape))
    head_rows = pltpu.VMEM((RW_HEADS, R, RW_HEAD), F32)
    return pl.pallas_call(
        functools.partial(_rwkv_sample_kernel, t_seq=t_seq),
        grid=(m // R,),
        in_specs=[
            pl.BlockSpec((R, RW_PACK), lambda i: (i, 0)),
            pl.BlockSpec((R, RW_PACK), lambda i: (i, 0)),
            pl.BlockSpec((SAMPLE_GB, RW_HEADS, RW_HEAD, RW_HEAD), lambda i: (i, 0, 0, 0)),
            full((1, RW_PACK)),
            full((1, RW_WIDTH)), full((64, RW_WIDTH)),
            full((1, RW_WIDTH)), full((64, RW_WIDTH)),
            full((128, RW_WIDTH)),
            full((1, RW_WIDTH)), full((1, RW_WIDTH)), full((1, RW_WIDTH)),
            full((1, RW_WIDTH)), full((1, RW_WIDTH)),
        ],
        out_specs=[
            pl.BlockSpec((R, RW_WIDTH), lambda i: (i, 0)),
            pl.BlockSpec((SAMPLE_GB, RW_HEADS, RW_HEAD, RW_HEAD), lambda i: (i, 0, 0, 0)),
        ],
        out_shape=[
            jax.ShapeDtypeStruct((m, RW_WIDTH), F32),
            jax.ShapeDtypeStruct((m // t_seq, RW_HEADS, RW_HEAD, RW_HEAD), F32),
        ],
        scratch_shapes=[head_rows] * 5 + [
            pltpu.VMEM((RW_HEADS, RW_HEAD, R), F32),
            head_rows,
        ],
        compiler_params=_cparams(("parallel",)),
        name="rwkv7_sample",
    )(p_rw, prev0, s0, prm["mu"], prm["w0"], prm["w_up"], prm["a0"], prm["a_up"], prm["g_up"],
      prm["k_k"], prm["k_a"], prm["r_k"], prm["ln_g"], prm["ln_b"])


def _sgu_kernel(p_ref, lng_ref, lnb_ref, w_ref, bs_ref, y_ref, v_ref):
    x = p_ref[...]
    z = 0.5 * x * (1.0 + lax.erf(x * (2.0 ** -0.5)))
    row = lax.broadcasted_iota(jnp.int32, (SG_CHUNK, SG_CHUNK), 0)
    col = lax.broadcasted_iota(jnp.int32, (SG_CHUNK, SG_CHUNK), 1)
    causal = row >= col
    for g in range(SG_GROUPS):
        sl = slice(g * 128, (g + 1) * 128)
        u = z[:, sl]
        vg = z[:, SG_WIDTH + g * 128:SG_WIDTH + (g + 1) * 128]
        mean = jnp.mean(vg, axis=-1, keepdims=True)
        vc = vg - mean
        var = jnp.mean(vc * vc, axis=-1, keepdims=True)
        vn = vc * lax.rsqrt(var + EPS) * lng_ref[:, sl] + lnb_ref[:, sl]
        wg = jnp.where(causal, w_ref[g], 0.0).astype(BF16)
        mixed = _dot(wg, vn.astype(BF16)) + bs_ref[:, g:g + 1]
        y_ref[:, sl] = u * mixed
        v_ref[:, sl] = vn


def _sgu(p_sg, ln_g, ln_b, w_mix, bs_col):
    m = p_sg.shape[0]
    return pl.pallas_call(
        _sgu_kernel,
        grid=(m // SG_CHUNK,),
        in_specs=[
            pl.BlockSpec((SG_CHUNK, SG_PACK), lambda i: (i, 0)),
            pl.BlockSpec((1, SG_WIDTH), lambda i: (0, 0)),
            pl.BlockSpec((1, SG_WIDTH), lambda i: (0, 0)),
            pl.BlockSpec((SG_GROUPS, SG_CHUNK, SG_CHUNK), lambda i: (0, 0, 0)),
            pl.BlockSpec((SG_CHUNK, SG_GROUPS), lambda i: (0, 0)),
        ],
        out_specs=[
            pl.BlockSpec((SG_CHUNK, SG_WIDTH), lambda i: (i, 0)),
            pl.BlockSpec((SG_CHUNK, SG_WIDTH), lambda i: (i, 0)),
        ],
        out_shape=[
            jax.ShapeDtypeStruct((m, SG_WIDTH), F32),
            jax.ShapeDtypeStruct((m, SG_WIDTH), F32),
        ],
        compiler_params=_cparams(("parallel",)),
        name="sgu",
    )(p_sg, ln_g, ln_b, w_mix, bs_col)


def _out_proj_kernel(x_ref, ym_ref, yr_ref, ys_ref, wm_ref, wr_ref, ws_ref, o_ref):
    acc = _dot(ym_ref[...].astype(BF16), wm_ref[...])
    acc = acc + _dot(yr_ref[...].astype(BF16), wr_ref[...])
    acc = acc + _dot(ys_ref[...].astype(BF16), ws_ref[...])
    o_ref[...] = x_ref[...] + acc


def _out_proj(x, ym, yr, ys, w_m, w_r, w_s, tm):
    m = x.shape[0]
    return pl.pallas_call(
        _out_proj_kernel,
        grid=(m // tm,),
        in_specs=[
            pl.BlockSpec((tm, D_MODEL), lambda i: (i, 0)),
            pl.BlockSpec((tm, ML_WIDTH), lambda i: (i, 0)),
            pl.BlockSpec((tm, RW_WIDTH), lambda i: (i, 0)),
            pl.BlockSpec((tm, SG_WIDTH), lambda i: (i, 0)),
            pl.BlockSpec((ML_WIDTH, D_MODEL), lambda i: (0, 0)),
            pl.BlockSpec((RW_WIDTH, D_MODEL), lambda i: (0, 0)),
            pl.BlockSpec((SG_WIDTH, D_MODEL), lambda i: (0, 0)),
        ],
        out_specs=pl.BlockSpec((tm, D_MODEL), lambda i: (i, 0)),
        out_shape=jax.ShapeDtypeStruct((m, D_MODEL), F32),
        compiler_params=_cparams(("parallel",)),
        name="out_proj",
    )(x, ym, yr, ys, w_m, w_r, w_s)


def _ffn_kernel(*refs, tm, n_f, seq_tiles, sample, final):
    if sample:
        (x_ref, g_ref, wg_ref, wv_ref, cwg_ref, cwv_ref, cbg_ref, cbv_ref, wd_ref, fg_ref,
         e1g_ref, e1v_ref, e2g_ref, e2v_ref,
         o_ref, hg_ref, hv_ref, xn_ref, acc_ref, cg_scr, cv_scr) = refs
    else:
        (x_ref, g_ref, wg_ref, wv_ref, cwg_ref, cwv_ref, cbg_ref, cbv_ref, wd_ref, fg_ref,
         o_ref, hg_ref, hv_ref, xn_ref, acc_ref, cg_scr, cv_scr) = refs
    i = pl.program_id(0)
    f = pl.program_id(1)

    @pl.when(f == 0)
    def _():
        xn_ref[...] = _rms(x_ref[...], g_ref[...]).astype(BF16)
        acc_ref[...] = jnp.zeros_like(acc_ref)

    xn = xn_ref[...]
    hg = _dot(xn, wg_ref[...])
    hv = _dot(xn, wv_ref[...])
    row = lax.broadcasted_iota(jnp.int32, (tm, 1), 0)

    def conv(h, cw_ref, cb_ref, carry_scr, e1_ref, e2_ref):
        r1 = pltpu.roll(h, 1, 0)
        r2 = pltpu.roll(h, 2, 0)
        if sample:
            t = row & 3
            hm1 = jnp.where(t >= 1, r1, e1_ref[...])
            hm2 = jnp.where(t >= 2, r2, e2_ref[...])
        else:
            first = (i % seq_tiles) == 0
            c = carry_scr[f]
            cm2 = jnp.where(first, 0.0, c[0:1, :])
            cm1 = jnp.where(first, 0.0, c[1:2, :])
            hm1 = jnp.where(row == 0, cm1, r1)
            hm2 = jnp.where(row == 0, cm2, jnp.where(row == 1, cm1, r2))
            carry_scr[f, 0:2, :] = h[tm - 2:tm, :]
        return cb_ref[...] + cw_ref[2:3, :] * h + cw_ref[1:2, :] * hm1 + cw_ref[0:1, :] * hm2

    if sample:
        cg = conv(hg, cwg_ref, cbg_ref, cg_scr, e1g_ref, e2g_ref)
        cv = conv(hv, cwv_ref, cbv_ref, cv_scr, e1v_ref, e2v_ref)
        hg_ref[...] = hg
        hv_ref[...] = hv
    else:
        cg = conv(hg, cwg_ref, cbg_ref, cg_scr, None, None)
        cv = conv(hv, cwv_ref, cbv_ref, cv_scr, None, None)
        hg_ref[0] = hg[tm - 2:tm, :]
        hv_ref[0] = hv[tm - 2:tm, :]
    act = (cg * _sigmoid(cg) * cv).astype(BF16)
    acc_ref[...] += _dot(act, wd_ref[...])

    @pl.when(f == n_f - 1)
    def _():
        out = x_ref[...] + acc_ref[...]
        if final:
            out = _rms(out, fg_ref[...])
        o_ref[...] = out


def _ffn(x, g, w_up_bf, conv_w, conv_b, w_down_bf, final_g, *, tm, seq_rows, sample, final, e1=None, e2=None):
    m = x.shape[0]
    tf = TF_FFN
    n_f = D_FF // tf
    n_m = m // tm
    seq_tiles = max(seq_rows // tm, 1)
    bg = m // seq_rows
    kern = functools.partial(_ffn_kernel, tm=tm, n_f=n_f, seq_tiles=seq_tiles, sample=sample, final=final)
    in_specs = [
        pl.BlockSpec((tm, D_MODEL), lambda i, f: (i, 0)),
        pl.BlockSpec((1, D_MODEL), lambda i, f: (0, 0)),
        pl.BlockSpec((D_MODEL, tf), lambda i, f: (0, f)),
        pl.BlockSpec((D_MODEL, tf), lambda i, f: (0, n_f + f)),
        pl.BlockSpec((3, tf), lambda i, f: (0, f)),
        pl.BlockSpec((3, tf), lambda i, f: (0, n_f + f)),
        pl.BlockSpec((1, tf), lambda i, f: (0, f)),
        pl.BlockSpec((1, tf), lambda i, f: (0, n_f + f)),
        pl.BlockSpec((tf, D_MODEL), lambda i, f: (f, 0)),
        pl.BlockSpec((1, D_MODEL), lambda i, f: (0, 0)),
    ]
    args = [x, g, w_up_bf, w_up_bf, conv_w, conv_w, conv_b, conv_b, w_down_bf, final_g]
    if sample:
        in_specs += [
            pl.BlockSpec((tm, tf), lambda i, f: (i, f)),
            pl.BlockSpec((tm, tf), lambda i, f: (i, n_f + f)),
            pl.BlockSpec((tm, tf), lambda i, f: (i, f)),
            pl.BlockSpec((tm, tf), lambda i, f: (i, n_f + f)),
        ]
        args += [e1, e1, e2, e2]
        h_specs = [pl.BlockSpec((tm, tf), lambda i, f: (i, f))] * 2
        h_shapes = [jax.ShapeDtypeStruct((m, D_FF), F32)] * 2
    else:
        h_specs = [pl.BlockSpec((1, 2, tf), lambda i, f: (i, 0, f))] * 2
        h_shapes = [jax.ShapeDtypeStruct((n_m, 2, D_FF), F32)] * 2
    return pl.pallas_call(
        kern,
        grid=(n_m, n_f),
        in_specs=in_specs,
        out_specs=[pl.BlockSpec((tm, D_MODEL), lambda i, f: (i, 0))] + h_specs,
        out_shape=[jax.ShapeDtypeStruct((m, D_MODEL), F32)] + h_shapes,
        scratch_shapes=[
            pltpu.VMEM((tm, D_MODEL), BF16),
            pltpu.VMEM((tm, D_MODEL), F32),
            pltpu.VMEM((n_f, 8, tf), F32),
            pltpu.VMEM((n_f, 8, tf), F32),
        ],
        compiler_params=_cparams(("arbitrary", "arbitrary")),
        name="conv_ffn",
    )(*args)


def _pack_w_in(w):
    ml = jnp.pad(w[:, :ML_COLS], ((0, 0), (0, ML_PACK - ML_COLS)))
    rw = jnp.pad(w[:, ML_COLS:ML_COLS + RW_COLS], ((0, 0), (0, RW_PACK - RW_COLS)))
    sg = w[:, ML_COLS + RW_COLS:]
    return jnp.concatenate([ml, rw, sg], axis=1).astype(BF16)


def _trunk(x3, states, lw, final_g, *, sample):
    bg, t, _ = x3.shape
    m = bg * t
    tp = SAMPLE_T_PAD if sample else t
    ml_chunk = SAMPLE_T_PAD if sample else ML_CHUNK
    x = x3.reshape(m, D_MODEL)
    outs = dict(C=[], n=[], m=[], S=[], sh=[], cb=[], v=[])
    for l in range(DEPTH):
        w = lw[l]
        st = states[l]
        p_ml, p_rw, p_sg = _in_proj(x, w["mix_g"], w["w_in"], min(m, 1024))

        def seq(a):
            a = a.reshape(bg, t, a.shape[-1])
            if sample:
                a = jnp.pad(a, ((0, 0), (0, tp - t), (0, 0)))
            return a

        p_ml3, p_sg3 = seq(p_ml), seq(p_sg)
        nc_ml = tp // ml_chunk
        g_rows = p_ml3[:, :, 4 * ML_WIDTH:4 * ML_WIDTH + 2 * ML_HEADS]
        g_rows = g_rows.reshape(bg, nc_ml, ml_chunk, 2 * ML_HEADS).transpose(0, 1, 3, 2)
        ym, c_new, n_new, m_new = _mlstm(p_ml3, g_rows, w["ml_b_col"], w["ml_b_row"], w["ml_norm_g"],
                                         st["C"], st["n"], st["m"], ml_chunk, t)
        if sample:
            prev0 = jnp.concatenate([st["sh"], jnp.zeros((bg, t - 1, RW_PACK), F32)], axis=1).reshape(m, RW_PACK)
            yr, s_new = _rwkv_sample(p_rw, prev0, st["S"], w, t)
        else:
            yr, s_new = _rwkv_prompt(p_rw.reshape(bg, t, RW_PACK), w)
            yr = yr.reshape(m, RW_WIDTH)
        ys, vrows = _sgu(p_sg3.reshape(bg * tp, SG_PACK), w["sg_ln_g"], w["sg_ln_b"],
                         w["sg_w_s"] if sample else w["sg_w_p"], w["sg_b_s"] if sample else w["sg_b_p"])

        def unseq(a):
            return a.reshape(bg, tp, a.shape[-1])[:, :t].reshape(m, a.shape[-1])

        x = _out_proj(x, unseq(ym), yr, unseq(ys), w["w_out_m"], w["w_out_r"], w["w_out_s"], 512)
        final = l == DEPTH - 1
        if sample:
            buf = st["cb"]
            e1 = jnp.concatenate([buf[:, 1:2], jnp.zeros((bg, t - 1, 2 * D_FF), F32)], axis=1).reshape(m, 2 * D_FF)
            e2 = jnp.concatenate([buf, jnp.zeros((bg, t - 2, 2 * D_FF), F32)], axis=1).reshape(m, 2 * D_FF)
            x, hg, hv = _ffn(x, w["ffn_g"], w["ffn_w_up"], w["conv_w"], w["conv_b"], w["w_down"], final_g,
                             tm=512, seq_rows=t, sample=True, final=final, e1=e1, e2=e2)
            hfull = jnp.concatenate([hg, hv], axis=-1).reshape(bg, t, 2 * D_FF)
            cb = hfull[:, t - 2:]
        else:
            x, hg, hv = _ffn(x, w["ffn_g"], w["ffn_w_up"], w["conv_w"], w["conv_b"], w["w_down"], final_g,
                             tm=512, seq_rows=t, sample=False, final=final)
            tiles = t // 512
            cb = jnp.concatenate([hg, hv], axis=-1)[tiles - 1::tiles]
        outs["C"].append(c_new)
        outs["n"].append(n_new[:, :, 0, :])
        outs["m"].append(m_new[:, :, 0, 0])
        outs["S"].append(s_new)
        outs["sh"].append(p_rw.reshape(bg, t, RW_PACK)[:, t - 1, :RW_COLS])
        outs["cb"].append(cb)
        outs["v"].append(vrows.reshape(bg, tp, SG_WIDTH)[:, :t])
    y = x.reshape(bg, t, D_MODEL)
    return y, {k: jnp.stack(v) for k, v in outs.items()}


def kernel(x_prompt, x_sample, state_mlstm_C, state_mlstm_n, state_mlstm_m, state_rwkv_S, state_rwkv_shift, state_ffn_conv, mix_norm_g, w_in, mlstm_b_i, mlstm_b_f, mlstm_norm_g, rwkv_mu, rwkv_w0, rwkv_w_up, rwkv_a0, rwkv_a_up, rwkv_g_up, rwkv_k_k, rwkv_k_a, rwkv_r_k, rwkv_ln_g, rwkv_ln_b, sgu_ln_g, sgu_ln_b, sgu_w, sgu_b, w_out, ffn_norm_g, ffn_w_up, ffn_conv_w, ffn_conv_b, ffn_w_down, final_norm_g):
    bp = x_prompt.shape[0]
    bs = x_sample.shape[0]
    lw = []
    reps = SG_CHUNK // SAMPLE_T_PAD
    for l in range(DEPTH):
        b_all = jnp.concatenate([mlstm_b_i[l], mlstm_b_f[l]])
        w8 = sgu_w[l][:, :SAMPLE_T_PAD, :SAMPLE_T_PAD]
        lw.append(dict(
            mix_g=mix_norm_g[l][None, :],
            w_in=_pack_w_in(w_in[l]),
            ml_b_col=b_all[:, None],
            ml_b_row=jnp.pad(b_all, (0, ML_WIDTH - 2 * ML_HEADS))[None, :],
            ml_norm_g=mlstm_norm_g[l][None, :],
            mu=jnp.pad(rwkv_mu[l], (0, RW_PACK - RW_COLS))[None, :],
            w0=rwkv_w0[l][None, :], w_up=rwkv_w_up[l].astype(BF16),
            a0=rwkv_a0[l][None, :], a_up=rwkv_a_up[l].astype(BF16),
            g_up=rwkv_g_up[l].astype(BF16),
            k_k=rwkv_k_k[l][None, :], k_a=rwkv_k_a[l][None, :],
            r_k=rwkv_r_k[l].reshape(1, RW_WIDTH),
            ln_g=rwkv_ln_g[l][None, :], ln_b=rwkv_ln_b[l][None, :],
            sg_ln_g=sgu_ln_g[l][None, :], sg_ln_b=sgu_ln_b[l][None, :],
            sg_w_p=sgu_w[l], sg_b_p=sgu_b[l].T,
            sg_w_s=jax.vmap(lambda a: jnp.kron(jnp.eye(reps, dtype=F32), a))(w8),
            sg_b_s=jnp.tile(sgu_b[l][:, :SAMPLE_T_PAD].T, (reps, 1)),
            w_out_m=w_out[l][:ML_WIDTH].astype(BF16),
            w_out_r=w_out[l][ML_WIDTH:ML_WIDTH + RW_WIDTH].astype(BF16),
            w_out_s=w_out[l][ML_WIDTH + RW_WIDTH:].astype(BF16),
            ffn_g=ffn_norm_g[l][None, :],
            ffn_w_up=ffn_w_up[l].astype(BF16),
            conv_w=ffn_conv_w[l], conv_b=ffn_conv_b[l][None, :],
            w_down=ffn_w_down[l].astype(BF16),
        ))
    fg = final_norm_g[None, :]

    zero_states = [dict(
        C=jnp.zeros((bp, ML_HEADS, ML_DIM, ML_DIM), F32),
        n=jnp.zeros((bp, ML_HEADS, 1, ML_DIM), F32),
        m=jnp.zeros((bp, ML_HEADS, 1, ML_DIM), F32),
        S=jnp.zeros((bp, RW_HEADS, RW_HEAD, RW_HEAD), F32),
        sh=jnp.zeros((bp, 1, RW_PACK), F32),
        cb=None,
    ) for _ in range(DEPTH)]
    samp_states = [dict(
        C=state_mlstm_C[l],
        n=state_mlstm_n[l][:, :, None, :],
        m=jnp.broadcast_to(state_mlstm_m[l][:, :, None, None], (bs, ML_HEADS, 1, ML_DIM)),
        S=state_rwkv_S[l],
        sh=jnp.pad(state_rwkv_shift[l], ((0, 0), (0, RW_PACK - RW_COLS)))[:, None, :],
        cb=state_ffn_conv[l],
    ) for l in range(DEPTH)]

    y_p, sp = _trunk(x_prompt, zero_states, lw, fg, sample=False)
    y_s, ss = _trunk(x_sample, samp_states, lw, fg, sample=True)
    return (y_p, y_s, sp["C"], sp["n"], sp["m"], sp["S"], sp["sh"], sp["cb"],
            ss["C"], ss["n"], ss["m"], ss["S"], ss["sh"], ss["cb"], ss["v"])
```
